```python
import jax, jax.numpy as jnp
from jax import lax
import numpy as np

D_MODEL = 2048
BATCH = 1
SEQ = 8192
DEPTH = 4

D_FF = 5632
N_EVEN = (DEPTH + 1) // 2
N_ODD = DEPTH // 2
GLA_HEADS = 4
GLA_DK = 128
GLA_DV = 256
GLA_QK = GLA_HEADS * GLA_DK
GLA_V = GLA_HEADS * GLA_DV
GLA_RANK = 16
GLA_GATE_TAU = 16.0
GLA_CHUNK = 64
CONV_DIM = D_MODEL // 2
CONV_WIDTH = 31
AB_SPLITS = (GLA_QK, 2 * GLA_QK, 2 * GLA_QK + GLA_V, 2 * GLA_QK + 2 * GLA_V,
             2 * GLA_QK + 2 * GLA_V + GLA_RANK)
AB_IN = 2 * GLA_QK + 2 * GLA_V + GLA_RANK + 2 * CONV_DIM
AB_MIX = GLA_V + CONV_DIM
FOX_HEADS = 16
FOX_DH = 128
FOX_D = FOX_HEADS * FOX_DH
FOX_IN = 4 * FOX_D + FOX_HEADS
FOX_BLOCK = 128
EPS = 1e-6

kernel_name = "hybrid_gla_conv_fox_macaron"


def rms_norm(x, g):
    xf = x.astype(jnp.float32)
    y = xf * lax.rsqrt(jnp.mean(xf * xf, axis=-1, keepdims=True) + EPS)
    return (y * g.astype(jnp.float32)).astype(x.dtype)


def layer_norm(x, g, b):
    xf = x.astype(jnp.float32)
    mu = jnp.mean(xf, axis=-1, keepdims=True)
    xc = xf - mu
    y = xc * lax.rsqrt(jnp.mean(xc * xc, axis=-1, keepdims=True) + EPS)
    return (y * g.astype(jnp.float32) + b.astype(jnp.float32)).astype(x.dtype)


def swiglu(x, w_gate, w_up, w_down):
    return (jax.nn.silu(x @ w_gate) * (x @ w_up)) @ w_down


def gla_chunked(q, k, v, log_a):
    b_, t_, h_, dk = q.shape
    dv = v.shape[-1]
    nc = t_ // GLA_CHUNK
    f32 = jnp.float32
    q = q.astype(f32).reshape(b_, nc, GLA_CHUNK, h_, dk) * (dk ** -0.5)
    k = k.astype(f32).reshape(b_, nc, GLA_CHUNK, h_, dk)
    v = v.astype(f32).reshape(b_, nc, GLA_CHUNK, h_, dv)
    cum = jnp.cumsum(log_a.astype(f32).reshape(b_, nc, GLA_CHUNK, h_, dk), axis=2)
    ref = cum[:, :, GLA_CHUNK // 2:GLA_CHUNK // 2 + 1]
    last = cum[:, :, -1:]
    a = jnp.einsum('bnchd,bnshd->bnhcs', q * jnp.exp(cum - ref), k * jnp.exp(ref - cum))
    causal = jnp.tril(jnp.ones((GLA_CHUNK, GLA_CHUNK), dtype=bool))
    a = jnp.where(causal, a, 0.0)
    o_intra = jnp.einsum('bnhcs,bnshv->bnchv', a, v)
    q_in = q * jnp.exp(cum)
    k_out = k * jnp.exp(last - cum)
    decay = jnp.exp(last[:, :, 0])

    def step(state, inp):
        qn, kn, vn, dn = inp
        o = jnp.einsum('bchd,bhdv->bchv', qn, state)
        state = state * dn[..., None] + jnp.einsum('bchd,bchv->bhdv', kn, vn)
        return state, o

    s0 = jnp.zeros((b_, h_, dk, dv), f32)
    xs = (jnp.moveaxis(q_in, 1, 0), jnp.moveaxis(k_out, 1, 0),
          jnp.moveaxis(v, 1, 0), jnp.moveaxis(decay, 1, 0))
    _, o_inter = lax.scan(step, s0, xs)
    o = o_intra + jnp.moveaxis(o_inter, 0, 1)
    return o.reshape(b_, t_, h_, dv)


def mixer_gla_conv(u, w_in, w_gk2, b_gk, gla_g, conv_w, conv_b, ln_g, ln_b, w_out):
    b_, t_, _ = u.shape
    z = u @ w_in
    q, k, v, g, gk_lr, conv_in = jnp.split(z, AB_SPLITS, axis=-1)
    log_a = jax.nn.log_sigmoid((gk_lr @ w_gk2 + b_gk).astype(jnp.float32)) / GLA_GATE_TAU
    hs = lambda a_, d_: a_.reshape(b_, t_, GLA_HEADS, d_)
    o = gla_chunked(hs(q, GLA_DK), hs(k, GLA_DK), hs(v, GLA_DV), hs(log_a, GLA_DK))
    o = rms_norm(o, gla_g).astype(u.dtype)
    o_gla = o.reshape(b_, t_, GLA_V) * jax.nn.silu(g)
    val, gate = jnp.split(conv_in, 2, axis=-1)
    c = val * jax.nn.sigmoid(gate)
    c = lax.conv_general_dilated(c, conv_w[:, None, :], window_strides=(1,),
                                 padding=[(CONV_WIDTH - 1, 0)],
                                 dimension_numbers=('NWC', 'WIO', 'NWC'),
                                 feature_group_count=CONV_DIM) + conv_b
    c = jax.nn.silu(layer_norm(c, ln_g, ln_b))
    return jnp.concatenate([o_gla, c], axis=-1) @ w_out


def mixer_fox(u, w_in, b_f, q_g, k_g, w_out):
    b_, t_, _ = u.shape
    f32 = jnp.float32
    z = u @ w_in
    q, k, v, og, f_logit = jnp.split(z, (FOX_D, 2 * FOX_D, 3 * FOX_D, 4 * FOX_D), axis=-1)
    heads = lambda a_: a_.reshape(b_, t_, FOX_HEADS, FOX_DH).transpose(0, 2, 1, 3)
    q = rms_norm(heads(q), q_g)
    k = rms_norm(heads(k), k_g)
    v = heads(v)
    log_f = jax.nn.log_sigmoid((f_logit + b_f).astype(f32))
    cum = jnp.cumsum(log_f, axis=1).transpose(0, 2, 1)
    nb = t_ // FOX_BLOCK
    qb = q.reshape(b_, FOX_HEADS, nb, FOX_BLOCK, FOX_DH).transpose(2, 0, 1, 3, 4)
    cb = cum.reshape(b_, FOX_HEADS, nb, FOX_BLOCK).transpose(2, 0, 1, 3)
    kpos = jnp.arange(t_)
    scale = FOX_DH ** -0.5

    def block(args):
        qi, ci, i = args
        s = jnp.einsum('bhqd,bhkd->bhqk', qi, k).astype(f32) * scale
        s = s + ci[..., None] - cum[:, :, None, :]
        qpos = i * FOX_BLOCK + jnp.arange(FOX_BLOCK)
        s = jnp.where(kpos[None, :] <= qpos[:, None], s, -jnp.inf)
        p = jax.nn.softmax(s, axis=-1).astype(v.dtype)
        return jnp.einsum('bhqk,bhkd->bhqd', p, v)

    o = lax.map(block, (qb, cb, jnp.arange(nb)))
    o = o.transpose(1, 0, 3, 2, 4).reshape(b_, t_, FOX_D)
    return (o * jax.nn.sigmoid(og)) @ w_out


def setup_inputs(seed: int = 0) -> dict:
    key = jax.random.key(seed)
    ks = iter(jax.random.split(key, 32))
    f32 = jnp.float32

    def w(shape, fan_in):
        return jax.random.normal(next(ks), shape, f32) * (fan_in ** -0.5)

    def gain(shape):
        return 1.0 + 0.02 * jax.random.normal(next(ks), shape, f32)

    def bias(shape, scale=0.02):
        return scale * jax.random.normal(next(ks), shape, f32)

    L, E, O = DEPTH, N_EVEN, N_ODD
    return {
        "x": jax.random.normal(next(ks), (BATCH, SEQ, D_MODEL), f32),
        "ffn1_norm": gain((L, D_MODEL)),
        "ffn1_gate": w((L, D_MODEL, D_FF), D_MODEL),
        "ffn1_up": w((L, D_MODEL, D_FF), D_MODEL),
        "ffn1_down": w((L, D_FF, D_MODEL), D_FF),
        "mix_norm": gain((L, D_MODEL)),
        "ffn2_norm": gain((L, D_MODEL)),
        "ffn2_gate": w((L, D_MODEL, D_FF), D_MODEL),
        "ffn2_up": w((L, D_MODEL, D_FF), D_MODEL),
        "ffn2_down": w((L, D_FF, D_MODEL), D_FF),
        "ab_w_in": w((E, D_MODEL, AB_IN), D_MODEL),
        "gla_w_gk2": w((E, GLA_RANK, GLA_QK), GLA_RANK),
        "gla_b_gk": bias((E, GLA_QK), 0.1),
        "gla_out_norm": gain((E, GLA_DV)),
        "conv_w": w((E, CONV_WIDTH, CONV_DIM), CONV_WIDTH),
        "conv_b": bias((E, CONV_DIM)),
        "conv_ln_g": gain((E, CONV_DIM)),
        "conv_ln_b": bias((E, CONV_DIM)),
        "ab_w_out": w((E, AB_MIX, D_MODEL), AB_MIX),
        "fox_w_in": w((O, D_MODEL, FOX_IN), D_MODEL),
        "fox_b_f": 3.0 + 0.5 * jax.random.normal(next(ks), (O, FOX_HEADS), f32),
        "fox_q_norm": gain((O, FOX_DH)),
        "fox_k_norm": gain((O, FOX_DH)),
        "fox_w_out": w((O, FOX_D, D_MODEL), FOX_D),
    }


def reference(x, ffn1_norm, ffn1_gate, ffn1_up, ffn1_down, mix_norm, ffn2_norm, ffn2_gate, ffn2_up,
              ffn2_down, ab_w_in, gla_w_gk2, gla_b_gk, gla_out_norm, conv_w, conv_b, conv_ln_g,
              conv_ln_b, ab_w_out, fox_w_in, fox_b_f, fox_q_norm, fox_k_norm, fox_w_out):
    h = x
    for layer in range(DEPTH):
        h = h + 0.5 * swiglu(rms_norm(h, ffn1_norm[layer]), ffn1_gate[layer], ffn1_up[layer], ffn1_down[layer])
        u = rms_norm(h, mix_norm[layer])
        if layer % 2 == 0:
            e = layer // 2
            h = h + mixer_gla_conv(u, ab_w_in[e], gla_w_gk2[e], gla_b_gk[e], gla_out_norm[e], conv_w[e],
                                   conv_b[e], conv_ln_g[e], conv_ln_b[e], ab_w_out[e])
        else:
            o = layer // 2
            h = h + mixer_fox(u, fox_w_in[o], fox_b_f[o], fox_q_norm[o], fox_k_norm[o], fox_w_out[o])
        h = h + 0.5 * swiglu(rms_norm(h, ffn2_norm[layer]), ffn2_gate[layer], ffn2_up[layer], ffn2_down[layer])
    return h
```

```python
import functools

import jax
import jax.numpy as jnp
from jax import lax
from jax.experimental import pallas as pl
from jax.experimental.pallas import tpu as pltpu

F32 = jnp.float32
BF16 = jnp.bfloat16

D_MODEL = 2048
SEQ = 8192
DEPTH = 4
D_FF = 5632
GLA_HEADS = 4
GLA_DK = 128
GLA_DV = 256
GLA_QK = GLA_HEADS * GLA_DK
GLA_V = GLA_HEADS * GLA_DV
GLA_RANK = 16
GLA_GATE_TAU = 16.0
GLA_CHUNK = 64
CONV_DIM = D_MODEL // 2
CONV_WIDTH = 31
AB_MAIN = 2 * GLA_QK + 2 * GLA_V
FOX_HEADS = 16
FOX_DH = 128
FOX_D = FOX_HEADS * FOX_DH
EPS = 1e-6

V7X_LANES = 128
V7X_SUBLANES = 8
V7X_VMEM_BYTES = 64 * 1024 * 1024
VMEM_LIMIT = 56 * 1024 * 1024

CONV_HALO = 32


def _params(semantics, vmem=VMEM_LIMIT):
    return pltpu.CompilerParams(dimension_semantics=semantics, vmem_limit_bytes=vmem)


def _rms_normed(x, gain):
    ms = jnp.mean(x * x, axis=-1, keepdims=True)
    return x * lax.rsqrt(ms + EPS) * gain


def _log_sigmoid(x):
    return jnp.minimum(x, 0.0) - jnp.log1p(jnp.exp(-jnp.abs(x)))


def _silu(x):
    return x * jax.nn.sigmoid(x)


def _lower_tri(n):
    r = lax.broadcasted_iota(jnp.int32, (n, n), 0)
    c = lax.broadcasted_iota(jnp.int32, (n, n), 1)
    return r >= c


FFN_TM = 512
FFN_TF = 512


def _ffn_kernel(h_ref, gain_ref, wg_ref, wu_ref, wd_ref, o_ref, xn_ref, acc_ref):
    f = pl.program_id(1)

    @pl.when(f == 0)
    def _():
        xn_ref[...] = _rms_normed(h_ref[...], gain_ref[...]).astype(BF16)
        acc_ref[...] = jnp.zeros_like(acc_ref)

    x = xn_ref[...]
    g = jnp.dot(x, wg_ref[...], preferred_element_type=F32)
    u = jnp.dot(x, wu_ref[...], preferred_element_type=F32)
    a = (_silu(g) * u).astype(BF16)
    acc_ref[...] += jnp.dot(a, wd_ref[...], preferred_element_type=F32)

    @pl.when(f == pl.num_programs(1) - 1)
    def _():
        o_ref[...] = h_ref[...] + 0.5 * acc_ref[...]


def _ffn(h, gain, wg, wu, wd):
    m, d = h.shape
    dff = wg.shape[1]
    return pl.pallas_call(
        _ffn_kernel,
        grid=(m // FFN_TM, dff // FFN_TF),
        in_specs=[
            pl.BlockSpec((FFN_TM, d), lambda i, f: (i, 0)),
            pl.BlockSpec((1, d), lambda i, f: (0, 0)),
            pl.BlockSpec((d, FFN_TF), lambda i, f: (0, f)),
            pl.BlockSpec((d, FFN_TF), lambda i, f: (0, f)),
            pl.BlockSpec((FFN_TF, d), lambda i, f: (f, 0)),
        ],
        out_specs=pl.BlockSpec((FFN_TM, d), lambda i, f: (i, 0)),
        out_shape=jax.ShapeDtypeStruct((m, d), F32),
        scratch_shapes=[pltpu.VMEM((FFN_TM, d), BF16), pltpu.VMEM((FFN_TM, d), F32)],
        compiler_params=_params(("parallel", "arbitrary")),
        name="ffn",
    )(h, gain, wg, wu, wd)


PROJ_TM = 1024
PROJ_TN = 512


def _proj_kernel(h_ref, gain_ref, w_ref, ws_ref, o_ref, os_ref, xn_ref):
    @pl.when(pl.program_id(1) == 0)
    def _():
        xn_ref[...] = _rms_normed(h_ref[...], gain_ref[...]).astype(BF16)
        os_ref[...] = jnp.dot(xn_ref[...], ws_ref[...], preferred_element_type=F32)

    o_ref[...] = jnp.dot(xn_ref[...], w_ref[...], preferred_element_type=F32)


def _proj(h, gain, w, w_side):
    m, d = h.shape
    n = w.shape[1]
    ns = w_side.shape[1]
    return pl.pallas_call(
        _proj_kernel,
        grid=(m // PROJ_TM, n // PROJ_TN),
        in_specs=[
            pl.BlockSpec((PROJ_TM, d), lambda i, j: (i, 0)),
            pl.BlockSpec((1, d), lambda i, j: (0, 0)),
            pl.BlockSpec((d, PROJ_TN), lambda i, j: (0, j)),
            pl.BlockSpec((d, ns), lambda i, j: (0, 0)),
        ],
        out_specs=[
            pl.BlockSpec((PROJ_TM, PROJ_TN), lambda i, j: (i, j)),
            pl.BlockSpec((PROJ_TM, ns), lambda i, j: (i, 0)),
        ],
        out_shape=[jax.ShapeDtypeStruct((m, n), F32), jax.ShapeDtypeStruct((m, ns), F32)],
        scratch_shapes=[pltpu.VMEM((PROJ_TM, d), BF16)],
        compiler_params=_params(("parallel", "arbitrary")),
        name="proj",
    )(h, gain, w, w_side)


OUT_TM = 1024
OUT_TN = 512


def _out_proj_kernel(a_ref, w_ref, h_ref, o_ref):
    o_ref[...] = h_ref[...] + jnp.dot(a_ref[...], w_ref[...], preferred_element_type=F32)


def _out_proj(a, w, h):
    m, k = a.shape
    n = w.shape[1]
    return pl.pallas_call(
        _out_proj_kernel,
        grid=(m // OUT_TM, n // OUT_TN),
        in_specs=[
            pl.BlockSpec((OUT_TM, k), lambda i, j: (i, 0)),
            pl.BlockSpec((k, OUT_TN), lambda i, j: (0, j)),
            pl.BlockSpec((OUT_TM, OUT_TN), lambda i, j: (i, j)),
        ],
        out_specs=pl.BlockSpec((OUT_TM, OUT_TN), lambda i, j: (i, j)),
        out_shape=jax.ShapeDtypeStruct((m, n), F32),
        compiler_params=_params(("parallel", "parallel")),
        name="out_proj",
    )(a, w, h)


MIX_TT = 512
CONV_RB = 32
CONV_LW = 512


def _gla_conv_kernel(q_ref, k_ref, v_ref, g_ref, cv_ref, cg_ref, lr_ref,
                     wgk_ref, bgk_ref, gnorm_ref, cw_ref, cb_ref, lng_ref, lnb_ref,
                     mix_ref, st_ref, cbuf_ref, la_ref):
    @pl.when(pl.program_id(0) == 0)
    def _():
        st_ref[...] = jnp.zeros_like(st_ref)
        cbuf_ref[0:CONV_HALO, :] = jnp.zeros((CONV_HALO, CONV_DIM), F32)

    gk = jnp.dot(lr_ref[...].astype(BF16), wgk_ref[...], preferred_element_type=F32) + bgk_ref[...]
    la_ref[...] = _log_sigmoid(gk) * (1.0 / GLA_GATE_TAU)

    tri = _lower_tri(GLA_CHUNK)
    tri_f = tri.astype(F32)
    gnorm = gnorm_ref[...]
    mid = GLA_CHUNK // 2

    def chunk(c, carry):
        r0 = pl.multiple_of(c * GLA_CHUNK, GLA_CHUNK)
        rows = pl.ds(r0, GLA_CHUNK)
        cum = jnp.dot(tri_f, la_ref[rows, :], preferred_element_type=F32,
                      precision=lax.Precision.HIGHEST)
        for h in range(GLA_HEADS):
            ks = slice(h * GLA_DK, (h + 1) * GLA_DK)
            vs = slice(h * GLA_DV, (h + 1) * GLA_DV)
            cum_h = cum[:, ks]
            ref = cum_h[mid:mid + 1, :]
            last = cum_h[GLA_CHUNK - 1:GLA_CHUNK, :]
            q = q_ref[rows, ks] * (GLA_DK ** -0.5)
            k = k_ref[rows, ks]
            v = v_ref[rows, vs].astype(BF16)
            qt = (q * jnp.exp(cum_h - ref)).astype(BF16)
            kt = (k * jnp.exp(ref - cum_h)).astype(BF16)
            a = lax.dot_general(qt, kt, (((1,), (1,)), ((), ())), preferred_element_type=F32)
            a = jnp.where(tri, a, 0.0).astype(BF16)
            o = jnp.dot(a, v, preferred_element_type=F32)
            st = st_ref[h]
            q_in = (q * jnp.exp(cum_h)).astype(BF16)
            o = o + lax.dot_general(q_in, st.astype(BF16), (((1,), (1,)), ((), ())),
                                    preferred_element_type=F32)
            k_out = (k * jnp.exp(last - cum_h)).astype(BF16)
            st_ref[h] = st * jnp.exp(last) + lax.dot_general(
                v, k_out, (((0,), (0,)), ((), ())), preferred_element_type=F32)
            on = _rms_normed(o, gnorm)
            mix_ref[rows, vs] = (on * _silu(g_ref[rows, vs])).astype(BF16)
        return carry

    lax.fori_loop(0, MIX_TT // GLA_CHUNK, chunk, 0)

    cbuf_ref[CONV_HALO:CONV_HALO + MIX_TT, :] = cv_ref[...] * jax.nn.sigmoid(cg_ref[...])
    cb = cb_ref[...]
    lng = lng_ref[...]
    lnb = lnb_ref[...]
    tap0 = CONV_HALO - (CONV_WIDTH - 1)

    def conv_rows(rb, carry):
        base = pl.multiple_of(rb * CONV_RB, CONV_RB)
        halves = []
        for lane0 in range(0, CONV_DIM, CONV_LW):
            ls = slice(lane0, lane0 + CONV_LW)
            acc = None
            for s in range(V7X_SUBLANES):
                part = None
                rows = CONV_RB + (V7X_SUBLANES if s else 0)
                for d in range(s, tap0 + CONV_WIDTH, V7X_SUBLANES):
                    if d < tap0:
                        continue
                    win = cbuf_ref[pl.ds(base + (d - s), rows), ls]
                    term = cw_ref[d - tap0:d - tap0 + 1, ls] * win
                    part = term if part is None else part + term
                part = part[s:s + CONV_RB, :]
                acc = part if acc is None else acc + part
            halves.append(acc)
        y = jnp.concatenate(halves, axis=1) + cb
        mu = jnp.mean(y, axis=-1, keepdims=True)
        yc = y - mu
        var = jnp.mean(yc * yc, axis=-1, keepdims=True)
        yn = yc * lax.rsqrt(var + EPS) * lng + lnb
        mix_ref[pl.ds(base, CONV_RB), GLA_V:GLA_V + CONV_DIM] = _silu(yn).astype(BF16)
        return carry

    lax.fori_loop(0, MIX_TT // CONV_RB, conv_rows, 0)
    cbuf_ref[0:CONV_HALO, :] = cbuf_ref[MIX_TT:MIX_TT + CONV_HALO, :]


def _gla_conv(z, zlr, wgk, bgk, gnorm, cw, cb, lng, lnb):
    m = z.shape[0]
    row = lambda shape: pl.BlockSpec(shape, lambda t: (0, 0))
    return pl.pallas_call(
        _gla_conv_kernel,
        grid=(m // MIX_TT,),
        in_specs=[
            pl.BlockSpec((MIX_TT, GLA_QK), lambda t: (t, 0)),
            pl.BlockSpec((MIX_TT, GLA_QK), lambda t: (t, 1)),
            pl.BlockSpec((MIX_TT, GLA_V), lambda t: (t, 1)),
            pl.BlockSpec((MIX_TT, GLA_V), lambda t: (t, 2)),
            pl.BlockSpec((MIX_TT, CONV_DIM), lambda t: (t, 3)),
            pl.BlockSpec((MIX_TT, CONV_DIM), lambda t: (t, 4)),
            pl.BlockSpec((MIX_TT, V7X_LANES), lambda t: (t, 0)),
            row((V7X_LANES, GLA_QK)),
            row((1, GLA_QK)),
            row((1, GLA_DV)),
            row((CONV_WIDTH, CONV_DIM)),
            row((1, CONV_DIM)),
            row((1, CONV_DIM)),
            row((1, CONV_DIM)),
        ],
        out_specs=pl.BlockSpec((MIX_TT, GLA_V + CONV_DIM), lambda t: (t, 0)),
        out_shape=jax.ShapeDtypeStruct((m, GLA_V + CONV_DIM), BF16),
        scratch_shapes=[
            pltpu.VMEM((GLA_HEADS, GLA_DV, GLA_DK), F32),
            pltpu.VMEM((CONV_HALO + MIX_TT, CONV_DIM), F32),
            pltpu.VMEM((MIX_TT, GLA_QK), F32),
        ],
        compiler_params=_params(("arbitrary",)),
        name="gla_conv",
    )(z, z, z, z, z, z, zlr, wgk, bgk, gnorm, cw, cb, lng, lnb)


PREP_TM = 256
CUM_TB = 256
ATT_TQ = 512
ATT_TK = 512


def _fox_prep_kernel(zq_ref, zk_ref, zv_ref, qg_ref, kg_ref, qn_ref, kn_ref, vb_ref):
    qg = qg_ref[...] * (FOX_DH ** -0.5)
    kg = kg_ref[...]
    for h in range(FOX_HEADS):
        hs = slice(h * FOX_DH, (h + 1) * FOX_DH)
        qn_ref[:, hs] = _rms_normed(zq_ref[:, hs], qg).astype(BF16)
        kn_ref[:, hs] = _rms_normed(zk_ref[:, hs], kg).astype(BF16)
    vb_ref[...] = zv_ref[...].astype(BF16)


def _fox_prep(z, qg, kg):
    m = z.shape[0]
    blk = lambda c: pl.BlockSpec((PREP_TM, FOX_D), lambda i, c=c: (i, c))
    vec = pl.BlockSpec((1, FOX_DH), lambda i: (0, 0))
    out = jax.ShapeDtypeStruct((m, FOX_D), BF16)
    return pl.pallas_call(
        _fox_prep_kernel,
        grid=(m // PREP_TM,),
        in_specs=[blk(0), blk(1), blk(2), vec, vec],
        out_specs=[blk(0), blk(0), blk(0)],
        out_shape=[out, out, out],
        compiler_params=_params(("parallel",)),
        name="fox_prep",
    )(z, z, z, qg, kg)


def _fox_cum_kernel(f_ref, b_ref, o_ref, carry_ref):
    @pl.when(pl.program_id(0) == 0)
    def _():
        carry_ref[...] = jnp.zeros_like(carry_ref)

    lf = _log_sigmoid(f_ref[...] + b_ref[...])
    cum = jnp.dot(_lower_tri(CUM_TB).astype(F32), lf, preferred_element_type=F32,
                  precision=lax.Precision.HIGHEST) + carry_ref[0:1, :]
    o_ref[...] = cum
    carry_ref[...] = jnp.broadcast_to(cum[CUM_TB - 1:CUM_TB, :], carry_ref.shape)


def _fox_cum(zf, bf):
    m, n = zf.shape
    return pl.pallas_call(
        _fox_cum_kernel,
        grid=(m // CUM_TB,),
        in_specs=[pl.BlockSpec((CUM_TB, n), lambda i: (i, 0)), pl.BlockSpec((1, n), lambda i: (0, 0))],
        out_specs=pl.BlockSpec((CUM_TB, n), lambda i: (i, 0)),
        out_shape=jax.ShapeDtypeStruct((m, n), F32),
        scratch_shapes=[pltpu.VMEM((V7X_SUBLANES, n), F32)],
        compiler_params=_params(("arbitrary",)),
        name="fox_cum",
    )(zf, bf)


def _fox_attn_kernel(q_ref, k_ref, v_ref, ccol_ref, crow_ref, og_ref, o_ref, m_ref, l_ref, acc_ref):
    i = pl.program_id(1)
    q = q_ref[...]
    ci = ccol_ref[...]
    m_ref[...] = jnp.full_like(m_ref, -jnp.inf)
    l_ref[...] = jnp.zeros_like(l_ref)
    acc_ref[...] = jnp.zeros_like(acc_ref)

    def block(j, masked):
        k0 = pl.multiple_of(j * ATT_TK, ATT_TK)
        k = k_ref[pl.ds(k0, ATT_TK), :]
        v = v_ref[pl.ds(k0, ATT_TK), :]
        cj = crow_ref[:, pl.ds(k0, ATT_TK)]
        s = lax.dot_general(q, k, (((1,), (1,)), ((), ())), preferred_element_type=F32)
        s = s + ci - cj
        if masked:
            s = jnp.where(_lower_tri(ATT_TQ), s, -jnp.inf)
        m_prev = m_ref[:, 0:1]
        m_new = jnp.maximum(m_prev, jnp.max(s, axis=-1, keepdims=True))
        alpha = jnp.exp(m_prev - m_new)
        p = jnp.exp(s - m_new)
        l_new = alpha * l_ref[:, 0:1] + jnp.sum(p, axis=-1, keepdims=True)
        acc_ref[...] = alpha * acc_ref[...] + jnp.dot(p.astype(BF16), v, preferred_element_type=F32)
        m_ref[...] = jnp.broadcast_to(m_new, m_ref.shape)
        l_ref[...] = jnp.broadcast_to(l_new, l_ref.shape)

    def body(j, carry):
        block(j, False)
        return carry

    lax.fori_loop(0, i, body, 0)
    block(i, True)
    o = acc_ref[...] / l_ref[:, 0:1]
    o_ref[...] = (o * jax.nn.sigmoid(og_ref[...])).astype(BF16)


def _fox_attn(qn, kn, vb, ccol, crow, z):
    m = qn.shape[0]
    og_col0 = 3 * FOX_HEADS
    return pl.pallas_call(
        _fox_attn_kernel,
        grid=(FOX_HEADS, m // ATT_TQ),
        in_specs=[
            pl.BlockSpec((ATT_TQ, FOX_DH), lambda h, i: (i, h)),
            pl.BlockSpec((m, FOX_DH), lambda h, i: (0, h)),
            pl.BlockSpec((m, FOX_DH), lambda h, i: (0, h)),
            pl.BlockSpec((None, ATT_TQ, 1), lambda h, i: (h, i, 0)),
            pl.BlockSpec((None, 1, m), lambda h, i: (h, 0, 0)),
            pl.BlockSpec((ATT_TQ, FOX_DH), lambda h, i: (i, og_col0 + h)),
        ],
        out_specs=pl.BlockSpec((ATT_TQ, FOX_DH), lambda h, i: (i, h)),
        out_shape=jax.ShapeDtypeStruct((m, FOX_D), BF16),
        scratch_shapes=[
            pltpu.VMEM((ATT_TQ, V7X_LANES), F32),
            pltpu.VMEM((ATT_TQ, V7X_LANES), F32),
            pltpu.VMEM((ATT_TQ, FOX_DH), F32),
        ],
        compiler_params=_params(("parallel", "arbitrary")),
        name="fox_attn",
    )(qn, kn, vb, ccol, crow, z)


def _pad_cols(w, n):
    return jnp.pad(w, ((0, 0), (0, n - w.shape[1])))


def _mixer_gla_conv(h, norm_gain, w_in, w_gk2, b_gk, gla_g, conv_w, conv_b, ln_g, ln_b, w_out):
    lr0 = AB_MAIN
    w_main = jnp.concatenate([w_in[:, :lr0], w_in[:, lr0 + GLA_RANK:]], axis=1).astype(BF16)
    w_lr = _pad_cols(w_in[:, lr0:lr0 + GLA_RANK], V7X_LANES).astype(BF16)
    w_gk2_p = jnp.pad(w_gk2, ((0, V7X_LANES - GLA_RANK), (0, 0))).astype(BF16)
    z, zlr = _proj(h, norm_gain[None, :], w_main, w_lr)
    mix = _gla_conv(z, zlr, w_gk2_p, b_gk[None, :], gla_g[None, :], conv_w,
                    conv_b[None, :], ln_g[None, :], ln_b[None, :])
    return _out_proj(mix, w_out.astype(BF16), h)


def _mixer_fox(h, norm_gain, w_in, b_f, q_g, k_g, w_out):
    m = h.shape[0]
    w_main = w_in[:, :4 * FOX_D].astype(BF16)
    w_f = _pad_cols(w_in[:, 4 * FOX_D:], V7X_LANES).astype(BF16)
    z, zf = _proj(h, norm_gain[None, :], w_main, w_f)
    qn, kn, vb = _fox_prep(z, q_g[None, :], k_g[None, :])
    cum = _fox_cum(zf, _pad_cols(b_f[None, :], V7X_LANES))
    cum_t = cum[:, :FOX_HEADS].T
    o = _fox_attn(qn, kn, vb, cum_t.reshape(FOX_HEADS, m, 1), cum_t.reshape(FOX_HEADS, 1, m), z)
    return _out_proj(o, w_out.astype(BF16), h)


def kernel(x, ffn1_norm, ffn1_gate, ffn1_up, ffn1_down, mix_norm, ffn2_norm, ffn2_gate, ffn2_up,
           ffn2_down, ab_w_in, gla_w_gk2, gla_b_gk, gla_out_norm, conv_w, conv_b, conv_ln_g,
           conv_ln_b, ab_w_out, fox_w_in, fox_b_f, fox_q_norm, fox_k_norm, fox_w_out):
    b, t, d = x.shape
    h = x.reshape(b * t, d)
    for layer in range(DEPTH):
        h = _ffn(h, ffn1_norm[layer][None, :], ffn1_gate[layer].astype(BF16),
                 ffn1_up[layer].astype(BF16), ffn1_down[layer].astype(BF16))
        if layer % 2 == 0:
            e = layer // 2
            h = _mixer_gla_conv(h, mix_norm[layer], ab_w_in[e], gla_w_gk2[e], gla_b_gk[e],
                                gla_out_norm[e], conv_w[e], conv_b[e], conv_ln_g[e], conv_ln_b[e],
                                ab_w_out[e])
        else:
            o = layer // 2
            h = _mixer_fox(h, mix_norm[layer], fox_w_in[o], fox_b_f[o], fox_q_norm[o],
                           fox_k_norm[o], fox_w_out[o])
        h = _ffn(h, ffn2_norm[layer][None, :], ffn2_gate[layer].astype(BF16),
                 ffn2_up[layer].astype(BF16), ffn2_down[layer].astype(BF16))
    return h.reshape(b, t, d)
```

```python
import jax
import jax.numpy as jnp
from jax import lax
from jax.experimental import pallas as pl
from jax.experimental.pallas import tpu as pltpu

F32 = jnp.float32
BF16 = jnp.bfloat16

D_MODEL = 2048
SEQ = 8192
DEPTH = 4
D_FF = 5632
GLA_HEADS = 4
GLA_DK = 128
GLA_DV = 256
GLA_QK = GLA_HEADS * GLA_DK
GLA_V = GLA_HEADS * GLA_DV
GLA_RANK = 16
GLA_GATE_TAU = 16.0
GLA_CHUNK = 64
CONV_DIM = D_MODEL // 2
CONV_WIDTH = 31
AB_MAIN = 2 * GLA_QK + 2 * GLA_V
FOX_HEADS = 16
FOX_DH = 128
FOX_D = FOX_HEADS * FOX_DH
EPS = 1e-6

V7X_LANES = 128
V7X_SUBLANES = 8
V7X_VMEM_BYTES = 64 * 1024 * 1024
VMEM_LIMIT = V7X_VMEM_BYTES * 7 // 8

CONV_HALO = 32


def _params(semantics, vmem=VMEM_LIMIT):
    return pltpu.CompilerParams(dimension_semantics=semantics, vmem_limit_bytes=vmem)


def _rms_normed(x, gain):
    ms = jnp.mean(x * x, axis=-1, keepdims=True)
    return x * lax.rsqrt(ms + EPS) * gain


def _log_sigmoid(x):
    return jnp.minimum(x, 0.0) - jnp.log1p(jnp.exp(-jnp.abs(x)))


def _silu(x):
    return x * jax.nn.sigmoid(x)


def _lower_tri(n):
    r = lax.broadcasted_iota(jnp.int32, (n, n), 0)
    c = lax.broadcasted_iota(jnp.int32, (n, n), 1)
    return r >= c


FFN_TM = 1024
FFN_TF = 512


def _ffn_kernel(h_ref, gain_ref, wg_ref, wu_ref, wd_ref, o_ref, xn_ref):
    f = pl.program_id(1)

    @pl.when(f == 0)
    def _():
        xn_ref[...] = _rms_normed(h_ref[...], gain_ref[...]).astype(BF16)
        o_ref[...] = jnp.zeros_like(o_ref)

    x = xn_ref[...]
    g = jnp.dot(x, wg_ref[...], preferred_element_type=F32)
    u = jnp.dot(x, wu_ref[...], preferred_element_type=F32)
    a = (_silu(g) * u).astype(BF16)
    o_ref[...] += jnp.dot(a, wd_ref[...], preferred_element_type=F32)

    @pl.when(f == pl.num_programs(1) - 1)
    def _():
        o_ref[...] = h_ref[...] + 0.5 * o_ref[...]


def _ffn(h, gain, wg, wu, wd, layer):
    m, d = h.shape
    dff = wg.shape[2]
    return pl.pallas_call(
        _ffn_kernel,
        grid=(m // FFN_TM, dff // FFN_TF),
        in_specs=[
            pl.BlockSpec((FFN_TM, d), lambda i, f: (i, 0)),
            pl.BlockSpec((None, 1, d), lambda i, f: (layer, 0, 0)),
            pl.BlockSpec((None, d, FFN_TF), lambda i, f: (layer, 0, f)),
            pl.BlockSpec((None, d, FFN_TF), lambda i, f: (layer, 0, f)),
            pl.BlockSpec((None, FFN_TF, d), lambda i, f: (layer, f, 0)),
        ],
        out_specs=pl.BlockSpec((FFN_TM, d), lambda i, f: (i, 0)),
        out_shape=jax.ShapeDtypeStruct((m, d), F32),
        scratch_shapes=[pltpu.VMEM((FFN_TM, d), BF16)],
        compiler_params=_params(("parallel", "arbitrary")),
        name="ffn",
    )(h, gain, wg, wu, wd)


PROJ_TM = 1024
PROJ_TN = 1024


def _proj_kernel(h_ref, gain_ref, w_ref, ws_ref, o_ref, os_ref, xn_ref):
    @pl.when(pl.program_id(1) == 0)
    def _():
        xn_ref[...] = _rms_normed(h_ref[...], gain_ref[...]).astype(BF16)
        os_ref[...] = jnp.dot(xn_ref[...], ws_ref[...], preferred_element_type=F32)

    o_ref[...] = jnp.dot(xn_ref[...], w_ref[...], preferred_element_type=F32)


def _proj(h, gain, w, w_side, layer, idx):
    m, d = h.shape
    n = w.shape[2]
    ns = w_side.shape[2]
    return pl.pallas_call(
        _proj_kernel,
        grid=(m // PROJ_TM, n // PROJ_TN),
        in_specs=[
            pl.BlockSpec((PROJ_TM, d), lambda i, j: (i, 0)),
            pl.BlockSpec((None, 1, d), lambda i, j: (layer, 0, 0)),
            pl.BlockSpec((None, d, PROJ_TN), lambda i, j: (idx, 0, j)),
            pl.BlockSpec((None, d, ns), lambda i, j: (idx, 0, 0)),
        ],
        out_specs=[
            pl.BlockSpec((PROJ_TM, PROJ_TN), lambda i, j: (i, j)),
            pl.BlockSpec((PROJ_TM, ns), lambda i, j: (i, 0)),
        ],
        out_shape=[jax.ShapeDtypeStruct((m, n), F32), jax.ShapeDtypeStruct((m, ns), F32)],
        scratch_shapes=[pltpu.VMEM((PROJ_TM, d), BF16)],
        compiler_params=_params(("parallel", "arbitrary")),
        name="proj",
    )(h, gain, w, w_side)


OUT_TM = 1024
OUT_TN = 1024


def _out_proj_kernel(a_ref, w_ref, h_ref, o_ref):
    o_ref[...] = h_ref[...] + jnp.dot(a_ref[...], w_ref[...], preferred_element_type=F32)


def _out_proj(a, w, h, idx):
    m, k = a.shape
    n = w.shape[2]
    return pl.pallas_call(
        _out_proj_kernel,
        grid=(m // OUT_TM, n // OUT_TN),
        in_specs=[
            pl.BlockSpec((OUT_TM, k), lambda i, j: (i, 0)),
            pl.BlockSpec((None, k, OUT_TN), lambda i, j: (idx, 0, j)),
            pl.BlockSpec((OUT_TM, OUT_TN), lambda i, j: (i, j)),
        ],
        out_specs=pl.BlockSpec((OUT_TM, OUT_TN), lambda i, j: (i, j)),
        out_shape=jax.ShapeDtypeStruct((m, n), F32),
        compiler_params=_params(("parallel", "parallel")),
        name="out_proj",
    )(a, w, h)


MIX_TT = 512
CONV_RB = 32
CONV_LW = 512


def _gla_conv_kernel(q_ref, k_ref, v_ref, g_ref, cv_ref, cg_ref, lr_ref,
                     wgk_ref, bgk_ref, gnorm_ref, cw_ref, cb_ref, lng_ref, lnb_ref,
                     mix_ref, st_ref, cbuf_ref, la_ref):
    @pl.when(pl.program_id(0) == 0)
    def _():
        st_ref[...] = jnp.zeros_like(st_ref)
        cbuf_ref[0:CONV_HALO, :] = jnp.zeros((CONV_HALO, CONV_DIM), F32)

    gk = jnp.dot(lr_ref[...].astype(BF16), wgk_ref[...], preferred_element_type=F32) + bgk_ref[...]
    la_ref[...] = _log_sigmoid(gk) * (1.0 / GLA_GATE_TAU)

    tri = _lower_tri(GLA_CHUNK)
    tri_f = tri.astype(F32)
    gnorm = gnorm_ref[...]
    mid = GLA_CHUNK // 2

    def chunk(c, carry):
        r0 = pl.multiple_of(c * GLA_CHUNK, GLA_CHUNK)
        rows = pl.ds(r0, GLA_CHUNK)
        cum = jnp.dot(tri_f, la_ref[rows, :], preferred_element_type=F32,
                      precision=lax.Precision.HIGHEST)
        for h in range(GLA_HEADS):
            ks = slice(h * GLA_DK, (h + 1) * GLA_DK)
            vs = slice(h * GLA_DV, (h + 1) * GLA_DV)
            cum_h = cum[:, ks]
            ref = cum_h[mid:mid + 1, :]
            last = cum_h[GLA_CHUNK - 1:GLA_CHUNK, :]
            q = q_ref[rows, ks] * (GLA_DK ** -0.5)
            k = k_ref[rows, ks]
            v = v_ref[rows, vs].astype(BF16)
            qt = (q * jnp.exp(cum_h - ref)).astype(BF16)
            kt = (k * jnp.exp(ref - cum_h)).astype(BF16)
            a = lax.dot_general(qt, kt, (((1,), (1,)), ((), ())), preferred_element_type=F32)
            a = jnp.where(tri, a, 0.0).astype(BF16)
            o = jnp.dot(a, v, preferred_element_type=F32)
            st = st_ref[h]
            q_in = (q * jnp.exp(cum_h)).astype(BF16)
            o = o + lax.dot_general(q_in, st.astype(BF16), (((1,), (1,)), ((), ())),
                                    preferred_element_type=F32)
            k_out = (k * jnp.exp(last - cum_h)).astype(BF16)
            st_ref[h] = st * jnp.exp(last) + lax.dot_general(
                v, k_out, (((0,), (0,)), ((), ())), preferred_element_type=F32)
            on = _rms_normed(o, gnorm)
            mix_ref[rows, vs] = (on * _silu(g_ref[rows, vs])).astype(BF16)
        return carry

    lax.fori_loop(0, MIX_TT // GLA_CHUNK, chunk, 0)

    cbuf_ref[CONV_HALO:CONV_HALO + MIX_TT, :] = cv_ref[...] * jax.nn.sigmoid(cg_ref[...])
    cb = cb_ref[...]
    lng = lng_ref[...]
    lnb = lnb_ref[...]
    tap0 = CONV_HALO - (CONV_WIDTH - 1)

    def conv_rows(rb, carry):
        base = pl.multiple_of(rb * CONV_RB, CONV_RB)
        halves = []
        for lane0 in range(0, CONV_DIM, CONV_LW):
            ls = slice(lane0, lane0 + CONV_LW)
            acc = None
            for s in range(V7X_SUBLANES):
                part = None
                rows = CONV_RB + (V7X_SUBLANES if s else 0)
                for d in range(s, tap0 + CONV_WIDTH, V7X_SUBLANES):
                    if d < tap0:
                        continue
                    win = cbuf_ref[pl.ds(base + (d - s), rows), ls]
                    term = cw_ref[d - tap0:d - tap0 + 1, ls] * win
                    part = term if part is None else part + term
                part = part[s:s + CONV_RB, :]
                acc = part if acc is None else acc + part
            halves.append(acc)
        y = jnp.concatenate(halves, axis=1) + cb
        mu = jnp.mean(y, axis=-1, keepdims=True)
        yc = y - mu
        var = jnp.mean(yc * yc, axis=-1, keepdims=True)
        yn = yc * lax.rsqrt(var + EPS) * lng + lnb
        mix_ref[pl.ds(base, CONV_RB), GLA_V:GLA_V + CONV_DIM] = _silu(yn).astype(BF16)
        return carry

    lax.fori_loop(0, MIX_TT // CONV_RB, conv_rows, 0)
    cbuf_ref[0:CONV_HALO, :] = cbuf_ref[MIX_TT:MIX_TT + CONV_HALO, :]


def _gla_conv(z, zlr, wgk, bgk, gnorm, cw, cb, lng, lnb):
    m = z.shape[0]
    row = lambda shape: pl.BlockSpec(shape, lambda t: (0, 0))
    return pl.pallas_call(
        _gla_conv_kernel,
        grid=(m // MIX_TT,),
        in_specs=[
            pl.BlockSpec((MIX_TT, GLA_QK), lambda t: (t, 0)),
            pl.BlockSpec((MIX_TT, GLA_QK), lambda t: (t, 1)),
            pl.BlockSpec((MIX_TT, GLA_V), lambda t: (t, 1)),
            pl.BlockSpec((MIX_TT, GLA_V), lambda t: (t, 2)),
            pl.BlockSpec((MIX_TT, CONV_DIM), lambda t: (t, 3)),
            pl.BlockSpec((MIX_TT, CONV_DIM), lambda t: (t, 4)),
            pl.BlockSpec((MIX_TT, V7X_LANES), lambda t: (t, 0)),
            row((V7X_LANES, GLA_QK)),
            row((1, GLA_QK)),
            row((1, GLA_DV)),
            row((CONV_WIDTH, CONV_DIM)),
            row((1, CONV_DIM)),
            row((1, CONV_DIM)),
            row((1, CONV_DIM)),
        ],
        out_specs=pl.BlockSpec((MIX_TT, GLA_V + CONV_DIM), lambda t: (t, 0)),
        out_shape=jax.ShapeDtypeStruct((m, GLA_V + CONV_DIM), BF16),
        scratch_shapes=[
            pltpu.VMEM((GLA_HEADS, GLA_DV, GLA_DK), F32),
            pltpu.VMEM((CONV_HALO + MIX_TT, CONV_DIM), F32),
            pltpu.VMEM((MIX_TT, GLA_QK), F32),
        ],
        compiler_params=_params(("arbitrary",)),
        name="gla_conv",
    )(z, z, z, z, z, z, zlr, wgk, bgk, gnorm, cw, cb, lng, lnb)


PREP_TM = 256
CUM_TB = 256
ATT_TQ = 512
ATT_TK = 512
FOX_DP = 2 * FOX_DH
LOG2E = 1.4426950408889634


def _split3(c):
    hi = c.astype(BF16).astype(F32)
    r = c - hi
    mid = r.astype(BF16).astype(F32)
    lo = (r - mid).astype(BF16).astype(F32)
    return hi, mid, lo


def _fox_prep_kernel(zq_ref, zk_ref, zv_ref, cum_ref, qg_ref, kg_ref, qp_ref, kp_ref, vp_ref):
    qg = qg_ref[...] * (FOX_DH ** -0.5 * LOG2E)
    kg = kg_ref[...]
    lane = lax.broadcasted_iota(jnp.int32, (PREP_TM, V7X_LANES), 1)
    ones = jnp.ones((PREP_TM, V7X_LANES), BF16)
    cum = cum_ref[...] * LOG2E
    for h in range(FOX_HEADS):
        hs = slice(h * FOX_DH, (h + 1) * FOX_DH)
        f0 = h * FOX_DP
        hi, mid, lo = _split3(cum[:, h:h + 1])
        qb = jnp.where(lane == 0, hi, jnp.where(lane == 1, mid, jnp.where(lane == 2, lo,
             jnp.where(lane < 6, 1.0, 0.0))))
        kb = jnp.where(lane < 3, 1.0, jnp.where(lane == 3, -hi, jnp.where(lane == 4, -mid,
             jnp.where(lane == 5, -lo, 0.0))))
        qp_ref[:, f0:f0 + FOX_DH] = _rms_normed(zq_ref[:, hs], qg).astype(BF16)
        qp_ref[:, f0 + FOX_DH:f0 + FOX_DP] = qb.astype(BF16)
        kp_ref[:, f0:f0 + FOX_DH] = _rms_normed(zk_ref[:, hs], kg).astype(BF16)
        kp_ref[:, f0 + FOX_DH:f0 + FOX_DP] = kb.astype(BF16)
        vp_ref[:, f0:f0 + FOX_DH] = zv_ref[:, hs].astype(BF16)
        vp_ref[:, f0 + FOX_DH:f0 + FOX_DP] = ones


def _fox_prep(z, cum, qg, kg):
    m = z.shape[0]
    blk = lambda c: pl.BlockSpec((PREP_TM, FOX_D), lambda i, c=c: (i, c))
    vec = pl.BlockSpec((1, FOX_DH), lambda i: (0, 0))
    out_spec = pl.BlockSpec((PREP_TM, FOX_HEADS * FOX_DP), lambda i: (i, 0))
    out = jax.ShapeDtypeStruct((m, FOX_HEADS * FOX_DP), BF16)
    return pl.pallas_call(
        _fox_prep_kernel,
        grid=(m // PREP_TM,),
        in_specs=[blk(0), blk(1), blk(2), pl.BlockSpec((PREP_TM, V7X_LANES), lambda i: (i, 0)), vec, vec],
        out_specs=[out_spec, out_spec, out_spec],
        out_shape=[out, out, out],
        compiler_params=_params(("parallel",)),
        name="fox_prep",
    )(z, z, z, cum, qg, kg)


def _fox_cum_kernel(f_ref, b_ref, o_ref, carry_ref):
    @pl.when(pl.program_id(0) == 0)
    def _():
        carry_ref[...] = jnp.zeros_like(carry_ref)

    lf = _log_sigmoid(f_ref[...] + b_ref[...])
    cum = jnp.dot(_lower_tri(CUM_TB).astype(F32), lf, preferred_element_type=F32,
                  precision=lax.Precision.HIGHEST) + carry_ref[0:1, :]
    o_ref[...] = cum
    carry_ref[...] = jnp.broadcast_to(cum[CUM_TB - 1:CUM_TB, :], carry_ref.shape)


def _fox_cum(zf, bf):
    m, n = zf.shape
    return pl.pallas_call(
        _fox_cum_kernel,
        grid=(m // CUM_TB,),
        in_specs=[pl.BlockSpec((CUM_TB, n), lambda i: (i, 0)), pl.BlockSpec((1, n), lambda i: (0, 0))],
        out_specs=pl.BlockSpec((CUM_TB, n), lambda i: (i, 0)),
        out_shape=jax.ShapeDtypeStruct((m, n), F32),
        scratch_shapes=[pltpu.VMEM((V7X_SUBLANES, n), F32)],
        compiler_params=_params(("arbitrary",)),
        name="fox_cum",
    )(zf, bf)


def _fox_attn_kernel(q_ref, k_ref, v_ref, og_ref, o_ref, m_ref, acc_ref):
    i = pl.program_id(1)
    q = q_ref[...]
    m_ref[...] = jnp.full_like(m_ref, -jnp.inf)
    acc_ref[...] = jnp.zeros_like(acc_ref)

    def logits(j):
        k0 = pl.multiple_of(j * ATT_TK, ATT_TK)
        return lax.dot_general(q, k_ref[pl.ds(k0, ATT_TK), :], (((1,), (1,)), ((), ())),
                               preferred_element_type=F32)

    def update(s, j):
        k0 = pl.multiple_of(j * ATT_TK, ATT_TK)
        m_prev = m_ref[...]
        m_new = jnp.maximum(m_prev, jnp.max(s, axis=-1, keepdims=True))
        alpha = jnp.exp2(m_prev - m_new)
        p = jnp.exp2(s - jnp.concatenate([m_new] * (ATT_TK // V7X_LANES), axis=1))
        pv = jnp.dot(p.astype(BF16), v_ref[pl.ds(k0, ATT_TK), :], preferred_element_type=F32)
        acc_ref[...] = jnp.concatenate([alpha] * (FOX_DP // V7X_LANES), axis=1) * acc_ref[...] + pv
        m_ref[...] = m_new

    def pair(jj, carry):
        s_a = logits(2 * jj)
        s_b = logits(2 * jj + 1)
        update(s_a, 2 * jj)
        update(s_b, 2 * jj + 1)
        return carry

    lax.fori_loop(0, i // 2, pair, 0)

    @pl.when(i % 2 == 1)
    def _():
        update(logits(i - 1), i - 1)

    update(jnp.where(_lower_tri(ATT_TQ), logits(i), -jnp.inf), i)
    acc = acc_ref[...]
    o = acc[:, :FOX_DH] / acc[:, FOX_DH:]
    o_ref[...] = (o * jax.nn.sigmoid(og_ref[...])).astype(BF16)


def _fox_attn(qp, kp, vp, z):
    m = qp.shape[0]
    og_col0 = 3 * FOX_HEADS
    return pl.pallas_call(
        _fox_attn_kernel,
        grid=(FOX_HEADS, m // ATT_TQ),
        in_specs=[
            pl.BlockSpec((ATT_TQ, FOX_DP), lambda h, i: (i, h)),
            pl.BlockSpec((m, FOX_DP), lambda h, i: (0, h)),
            pl.BlockSpec((m, FOX_DP), lambda h, i: (0, h)),
            pl.BlockSpec((ATT_TQ, FOX_DH), lambda h, i: (i, og_col0 + h)),
        ],
        out_specs=pl.BlockSpec((ATT_TQ, FOX_DH), lambda h, i: (i, h)),
        out_shape=jax.ShapeDtypeStruct((m, FOX_D), BF16),
        scratch_shapes=[
            pltpu.VMEM((ATT_TQ, V7X_LANES), F32),
            pltpu.VMEM((ATT_TQ, FOX_DP), F32),
        ],
        compiler_params=_params(("parallel", "arbitrary")),
        name="fox_attn",
    )(qp, kp, vp, z)


def _pad_last(w, n):
    return jnp.pad(w, [(0, 0)] * (w.ndim - 1) + [(0, n - w.shape[-1])])


def _mixer_gla_conv(h, e, layer, p):
    z, zlr = _proj(h, p["mix_norm"], p["ab_w_main"], p["ab_w_lr"], layer, e)
    mix = _gla_conv(z, zlr, p["gla_w_gk2"][e], p["gla_b_gk"][e][None, :], p["gla_out_norm"][e][None, :],
                    p["conv_w"][e], p["conv_b"][e][None, :], p["conv_ln_g"][e][None, :],
                    p["conv_ln_b"][e][None, :])
    return _out_proj(mix, p["ab_w_out"], h, e)


def _mixer_fox(h, o, layer, p):
    z, zf = _proj(h, p["mix_norm"], p["fox_w_main"], p["fox_w_f"], layer, o)
    cum = _fox_cum(zf, p["fox_b_f"][o][None, :])
    qp, kp, vp = _fox_prep(z, cum, p["fox_q_norm"][o][None, :], p["fox_k_norm"][o][None, :])
    att = _fox_attn(qp, kp, vp, z)
    return _out_proj(att, p["fox_w_out"], h, o)


def kernel(x, ffn1_norm, ffn1_gate, ffn1_up, ffn1_down, mix_norm, ffn2_norm, ffn2_gate, ffn2_up,
           ffn2_down, ab_w_in, gla_w_gk2, gla_b_gk, gla_out_norm, conv_w, conv_b, conv_ln_g,
           conv_ln_b, ab_w_out, fox_w_in, fox_b_f, fox_q_norm, fox_k_norm, fox_w_out):
    b, t, d = x.shape
    assert b == 1, "the time recurrences assume one sequence"
    lr0 = AB_MAIN
    p = dict(
        mix_norm=mix_norm[:, None, :],
        ab_w_main=jnp.concatenate([ab_w_in[:, :, :lr0], ab_w_in[:, :, lr0 + GLA_RANK:]], axis=2).astype(BF16),
        ab_w_lr=_pad_last(ab_w_in[:, :, lr0:lr0 + GLA_RANK], V7X_LANES).astype(BF16),
        gla_w_gk2=jnp.pad(gla_w_gk2, ((0, 0), (0, V7X_LANES - GLA_RANK), (0, 0))).astype(BF16),
        gla_b_gk=gla_b_gk, gla_out_norm=gla_out_norm, conv_w=conv_w, conv_b=conv_b,
        conv_ln_g=conv_ln_g, conv_ln_b=conv_ln_b, ab_w_out=ab_w_out.astype(BF16),
        fox_w_main=fox_w_in[:, :, :4 * FOX_D].astype(BF16),
        fox_w_f=_pad_last(fox_w_in[:, :, 4 * FOX_D:], V7X_LANES).astype(BF16),
        fox_b_f=_pad_last(fox_b_f, V7X_LANES), fox_q_norm=fox_q_norm, fox_k_norm=fox_k_norm,
        fox_w_out=fox_w_out.astype(BF16),
    )
    ffn1 = (ffn1_norm[:, None, :], ffn1_gate.astype(BF16), ffn1_up.astype(BF16), ffn1_down.astype(BF16))
    ffn2 = (ffn2_norm[:, None, :], ffn2_gate.astype(BF16), ffn2_up.astype(BF16), ffn2_down.astype(BF16))
    h = x.reshape(b * t, d)
    for layer in range(DEPTH):
        h = _ffn(h, *ffn1, layer)
        if layer % 2 == 0:
            h = _mixer_gla_conv(h, layer // 2, layer, p)
        else:
            h = _mixer_fox(h, layer // 2, layer, p)
        h = _ffn(h, *ffn2, layer)
    return h.reshape(b, t, d)
```

```python
import functools

import jax
import jax.numpy as jnp
from jax import lax
from jax.experimental import pallas as pl
from jax.experimental.pallas import tpu as pltpu

F32 = jnp.float32
BF16 = jnp.bfloat16

D_MODEL = 2048
SEQ = 8192
DEPTH = 4
D_FF = 5632
GLA_HEADS = 4
GLA_DK = 128
GLA_DV = 256
GLA_QK = GLA_HEADS * GLA_DK
GLA_V = GLA_HEADS * GLA_DV
GLA_RANK = 16
GLA_GATE_TAU = 16.0
GLA_CHUNK = 64
CONV_DIM = D_MODEL // 2
CONV_WIDTH = 31
AB_MAIN = 2 * GLA_QK + 2 * GLA_V
FOX_HEADS = 16
FOX_DH = 128
FOX_D = FOX_HEADS * FOX_DH
EPS = 1e-6

V7X_LANES = 128
V7X_SUBLANES = 8
V7X_VMEM_BYTES = 64 * 1024 * 1024
VMEM_LIMIT = V7X_VMEM_BYTES * 7 // 8

CONV_HALO = 32


def _params(semantics, vmem=VMEM_LIMIT):
    return pltpu.CompilerParams(dimension_semantics=semantics, vmem_limit_bytes=vmem)


def _rms_normed(x, gain):
    ms = jnp.mean(x * x, axis=-1, keepdims=True)
    return x * lax.rsqrt(ms + EPS) * gain


def _log_sigmoid(x):
    return jnp.minimum(x, 0.0) - jnp.log1p(jnp.exp(-jnp.abs(x)))


def _silu(x):
    return x * jax.nn.sigmoid(x)


def _lower_tri(n):
    r = lax.broadcasted_iota(jnp.int32, (n, n), 0)
    c = lax.broadcasted_iota(jnp.int32, (n, n), 1)
    return r >= c


FFN_TM = 1024
FFN_TF = 512


def _ffn_kernel(h_ref, gain_ref, wg_ref, wu_ref, wd_ref, o_ref, xn_ref):
    f = pl.program_id(1)

    @pl.when(f == 0)
    def _():
        xn_ref[...] = _rms_normed(h_ref[...], gain_ref[...]).astype(BF16)
        o_ref[...] = jnp.zeros_like(o_ref)

    x = xn_ref[...]
    g = jnp.dot(x, wg_ref[...], preferred_element_type=F32)
    u = jnp.dot(x, wu_ref[...], preferred_element_type=F32)
    a = (_silu(g) * u).astype(BF16)
    o_ref[...] += jnp.dot(a, wd_ref[...], preferred_element_type=F32)

    @pl.when(f == pl.num_programs(1) - 1)
    def _():
        o_ref[...] = h_ref[...] + 0.5 * o_ref[...]


def _ffn(h, gain, wg, wu, wd, layer):
    m, d = h.shape
    dff = wg.shape[2]
    return pl.pallas_call(
        _ffn_kernel,
        grid=(m // FFN_TM, dff // FFN_TF),
        in_specs=[
            pl.BlockSpec((FFN_TM, d), lambda i, f: (i, 0)),
            pl.BlockSpec((None, 1, d), lambda i, f: (layer, 0, 0)),
            pl.BlockSpec((None, d, FFN_TF), lambda i, f: (layer, 0, f)),
            pl.BlockSpec((None, d, FFN_TF), lambda i, f: (layer, 0, f)),
            pl.BlockSpec((None, FFN_TF, d), lambda i, f: (layer, f, 0)),
        ],
        out_specs=pl.BlockSpec((FFN_TM, d), lambda i, f: (i, 0)),
        out_shape=jax.ShapeDtypeStruct((m, d), F32),
        scratch_shapes=[pltpu.VMEM((FFN_TM, d), BF16)],
        compiler_params=_params(("parallel", "arbitrary")),
        name="ffn",
    )(h, gain, wg, wu, wd)


PROJ_TM = 1024
PROJ_TN = 1024


def _proj_kernel(has_side, h_ref, gain_ref, w_ref, *rest):
    if has_side:
        ws_ref, o_ref, os_ref, xn_ref = rest
    else:
        o_ref, xn_ref = rest

    @pl.when(pl.program_id(1) == 0)
    def _():
        xn_ref[...] = _rms_normed(h_ref[...], gain_ref[...]).astype(BF16)
        if has_side:
            os_ref[...] = jnp.dot(xn_ref[...], ws_ref[...], preferred_element_type=F32)

    o_ref[...] = jnp.dot(xn_ref[...], w_ref[...], preferred_element_type=F32)


def _proj(h, gain, layer, w, idx, n, w_side=None):
    m, d = h.shape
    in_specs = [
        pl.BlockSpec((PROJ_TM, d), lambda i, j: (i, 0)),
        pl.BlockSpec((None, 1, d), lambda i, j: (layer, 0, 0)),
        pl.BlockSpec((None, d, PROJ_TN), lambda i, j: (idx, 0, j)),
    ]
    out_specs = [pl.BlockSpec((PROJ_TM, PROJ_TN), lambda i, j: (i, j))]
    out_shape = [jax.ShapeDtypeStruct((m, n), F32)]
    args = [h, gain, w]
    if w_side is not None:
        ns = w_side.shape[2]
        in_specs.append(pl.BlockSpec((None, d, ns), lambda i, j: (idx, 0, 0)))
        out_specs.append(pl.BlockSpec((PROJ_TM, ns), lambda i, j: (i, 0)))
        out_shape.append(jax.ShapeDtypeStruct((m, ns), F32))
        args.append(w_side)
    return pl.pallas_call(
        functools.partial(_proj_kernel, w_side is not None),
        grid=(m // PROJ_TM, n // PROJ_TN),
        in_specs=in_specs,
        out_specs=out_specs,
        out_shape=out_shape,
        scratch_shapes=[pltpu.VMEM((PROJ_TM, d), BF16)],
        compiler_params=_params(("parallel", "arbitrary")),
        name="proj",
    )(*args)


OUT_TM = 512
OUT_TN = 2048


def _out_proj_kernel(a_ref, w_ref, h_ref, o_ref):
    o_ref[...] = h_ref[...] + jnp.dot(a_ref[...], w_ref[...], preferred_element_type=F32)


def _out_proj(a, w, h, idx):
    m, k = a.shape
    n = w.shape[2]
    return pl.pallas_call(
        _out_proj_kernel,
        grid=(m // OUT_TM, n // OUT_TN),
        in_specs=[
            pl.BlockSpec((OUT_TM, k), lambda i, j: (i, 0)),
            pl.BlockSpec((None, k, OUT_TN), lambda i, j: (idx, 0, j)),
            pl.BlockSpec((OUT_TM, OUT_TN), lambda i, j: (i, j)),
        ],
        out_specs=pl.BlockSpec((OUT_TM, OUT_TN), lambda i, j: (i, j)),
        out_shape=jax.ShapeDtypeStruct((m, n), F32),
        compiler_params=_params(("parallel", "parallel")),
        name="out_proj",
    )(a, w, h)


MIX_TT = 512
CONV_RB = 32
CONV_LW = 512


def _gla_conv_kernel(q_ref, k_ref, v_ref, g_ref, cv_ref, cg_ref, lr_ref,
                     wgk_ref, bgk_ref, gnorm_ref, cw_ref, cb_ref, lng_ref, lnb_ref,
                     mix_ref, st_ref, cbuf_ref, la_ref):
    @pl.when(pl.program_id(0) == 0)
    def _():
        st_ref[...] = jnp.zeros_like(st_ref)
        cbuf_ref[0:CONV_HALO, :] = jnp.zeros((CONV_HALO, CONV_DIM), F32)

    gk = jnp.dot(lr_ref[...].astype(BF16), wgk_ref[...], preferred_element_type=F32) + bgk_ref[...]
    la_ref[...] = _log_sigmoid(gk) * (1.0 / GLA_GATE_TAU)

    tri = _lower_tri(GLA_CHUNK)
    tri_f = tri.astype(F32)
    gnorm = gnorm_ref[...]
    mid = GLA_CHUNK // 2

    def chunk(c, carry):
        r0 = pl.multiple_of(c * GLA_CHUNK, GLA_CHUNK)
        rows = pl.ds(r0, GLA_CHUNK)
        cum = jnp.dot(tri_f, la_ref[rows, :], preferred_element_type=F32,
                      precision=lax.Precision.HIGHEST)
        for h in range(GLA_HEADS):
            ks = slice(h * GLA_DK, (h + 1) * GLA_DK)
            vs = slice(h * GLA_DV, (h + 1) * GLA_DV)
            cum_h = cum[:, ks]
            ref = cum_h[mid:mid + 1, :]
            last = cum_h[GLA_CHUNK - 1:GLA_CHUNK, :]
            q = q_ref[rows, ks] * (GLA_DK ** -0.5)
            k = k_ref[rows, ks]
            v = v_ref[rows, vs].astype(BF16)
            qt = (q * jnp.exp(cum_h - ref)).astype(BF16)
            kt = (k * jnp.exp(ref - cum_h)).astype(BF16)
            a = lax.dot_general(qt, kt, (((1,), (1,)), ((), ())), preferred_element_type=F32)
            a = jnp.where(tri, a, 0.0).astype(BF16)
            o = jnp.dot(a, v, preferred_element_type=F32)
            st = st_ref[h]
            q_in = (q * jnp.exp(cum_h)).astype(BF16)
            o = o + lax.dot_general(q_in, st.astype(BF16), (((1,), (1,)), ((), ())),
                                    preferred_element_type=F32)
            k_out = (k * jnp.exp(last - cum_h)).astype(BF16)
            st_ref[h] = st * jnp.exp(last) + lax.dot_general(
                v, k_out, (((0,), (0,)), ((), ())), preferred_element_type=F32)
            on = _rms_normed(o, gnorm)
            mix_ref[rows, vs] = (on * _silu(g_ref[rows, vs])).astype(BF16)
        return carry

    lax.fori_loop(0, MIX_TT // GLA_CHUNK, chunk, 0)

    cbuf_ref[CONV_HALO:CONV_HALO + MIX_TT, :] = cv_ref[...] * jax.nn.sigmoid(cg_ref[...])
    cb = cb_ref[...]
    lng = lng_ref[...]
    lnb = lnb_ref[...]
    tap0 = CONV_HALO - (CONV_WIDTH - 1)

    def conv_rows(rb, carry):
        base = pl.multiple_of(rb * CONV_RB, CONV_RB)
        halves = []
        for lane0 in range(0, CONV_DIM, CONV_LW):
            ls = slice(lane0, lane0 + CONV_LW)
            acc = None
            for s in range(V7X_SUBLANES):
                part = None
                rows = CONV_RB + (V7X_SUBLANES if s else 0)
                for d in range(s, tap0 + CONV_WIDTH, V7X_SUBLANES):
                    if d < tap0:
                        continue
                    win = cbuf_ref[pl.ds(base + (d - s), rows), ls]
                    term = cw_ref[d - tap0:d - tap0 + 1, ls] * win
                    part = term if part is None else part + term
                part = part[s:s + CONV_RB, :]
                acc = part if acc is None else acc + part
            halves.append(acc)
        y = jnp.concatenate(halves, axis=1) + cb
        mu = jnp.mean(y, axis=-1, keepdims=True)
        yc = y - mu
        var = jnp.mean(yc * yc, axis=-1, keepdims=True)
        yn = yc * lax.rsqrt(var + EPS) * lng + lnb
        mix_ref[pl.ds(base, CONV_RB), GLA_V:GLA_V + CONV_DIM] = _silu(yn).astype(BF16)
        return carry

    lax.fori_loop(0, MIX_TT // CONV_RB, conv_rows, 0)
    cbuf_ref[0:CONV_HALO, :] = cbuf_ref[MIX_TT:MIX_TT + CONV_HALO, :]


def _gla_conv(z, zc, zlr, wgk, bgk, gnorm, cw, cb, lng, lnb):
    m = z.shape[0]
    row = lambda shape: pl.BlockSpec(shape, lambda t: (0, 0))
    return pl.pallas_call(
        _gla_conv_kernel,
        grid=(m // MIX_TT,),
        in_specs=[
            pl.BlockSpec((MIX_TT, GLA_QK), lambda t: (t, 0)),
            pl.BlockSpec((MIX_TT, GLA_QK), lambda t: (t, 1)),
            pl.BlockSpec((MIX_TT, GLA_V), lambda t: (t, 1)),
            pl.BlockSpec((MIX_TT, GLA_V), lambda t: (t, 2)),
            pl.BlockSpec((MIX_TT, CONV_DIM), lambda t: (t, 0)),
            pl.BlockSpec((MIX_TT, CONV_DIM), lambda t: (t, 1)),
            pl.BlockSpec((MIX_TT, V7X_LANES), lambda t: (t, 0)),
            row((V7X_LANES, GLA_QK)),
            row((1, GLA_QK)),
            row((1, GLA_DV)),
            row((CONV_WIDTH, CONV_DIM)),
            row((1, CONV_DIM)),
            row((1, CONV_DIM)),
            row((1, CONV_DIM)),
        ],
        out_specs=pl.BlockSpec((MIX_TT, GLA_V + CONV_DIM), lambda t: (t, 0)),
        out_shape=jax.ShapeDtypeStruct((m, GLA_V + CONV_DIM), BF16),
        scratch_shapes=[
            pltpu.VMEM((GLA_HEADS, GLA_DV, GLA_DK), F32),
            pltpu.VMEM((CONV_HALO + MIX_TT, CONV_DIM), F32),
            pltpu.VMEM((MIX_TT, GLA_QK), F32),
        ],
        compiler_params=_params(("arbitrary",)),
        name="gla_conv",
    )(z, z, z, z, zc, zc, zlr, wgk, bgk, gnorm, cw, cb, lng, lnb)


PREP_TM = 256
CUM_TB = 256
ATT_TQ = 1024
ATT_TK = 512
FOX_DP = 2 * FOX_DH
LOG2E = 1.4426950408889634


def _split3(c):
    hi = c.astype(BF16).astype(F32)
    r = c - hi
    mid = r.astype(BF16).astype(F32)
    lo = (r - mid).astype(BF16).astype(F32)
    return hi, mid, lo


def _fox_prep_kernel(zq_ref, zk_ref, zv_ref, cum_ref, qg_ref, kg_ref, qp_ref, kp_ref, vp_ref):
    qg = qg_ref[...] * (FOX_DH ** -0.5 * LOG2E)
    kg = kg_ref[...]
    lane = lax.broadcasted_iota(jnp.int32, (PREP_TM, V7X_LANES), 1)
    ones = jnp.ones((PREP_TM, V7X_LANES), BF16)
    cum = cum_ref[...] * LOG2E
    for h in range(FOX_HEADS):
        hs = slice(h * FOX_DH, (h + 1) * FOX_DH)
        f0 = h * FOX_DP
        hi, mid, lo = _split3(jnp.broadcast_to(cum[:, h:h + 1], (PREP_TM, V7X_LANES)))
        qb = jnp.where(lane == 0, hi, jnp.where(lane == 1, mid, jnp.where(lane == 2, lo,
             jnp.where(lane < 6, 1.0, 0.0))))
        kb = jnp.where(lane < 3, 1.0, jnp.where(lane == 3, -hi, jnp.where(lane == 4, -mid,
             jnp.where(lane == 5, -lo, 0.0))))
        qp_ref[:, f0:f0 + FOX_DH] = _rms_normed(zq_ref[:, hs], qg).astype(BF16)
        qp_ref[:, f0 + FOX_DH:f0 + FOX_DP] = qb.astype(BF16)
        kp_ref[:, f0:f0 + FOX_DH] = _rms_normed(zk_ref[:, hs], kg).astype(BF16)
        kp_ref[:, f0 + FOX_DH:f0 + FOX_DP] = kb.astype(BF16)
        vp_ref[:, f0:f0 + FOX_DH] = zv_ref[:, hs].astype(BF16)
        vp_ref[:, f0 + FOX_DH:f0 + FOX_DP] = ones


def _fox_prep(z, cum, qg, kg):
    m = z.shape[0]
    blk = lambda c: pl.BlockSpec((PREP_TM, FOX_D), lambda i, c=c: (i, c))
    vec = pl.BlockSpec((1, FOX_DH), lambda i: (0, 0))
    out_spec = pl.BlockSpec((PREP_TM, FOX_HEADS * FOX_DP), lambda i: (i, 0))
    out = jax.ShapeDtypeStruct((m, FOX_HEADS * FOX_DP), BF16)
    return pl.pallas_call(
        _fox_prep_kernel,
        grid=(m // PREP_TM,),
        in_specs=[blk(0), blk(1), blk(2), pl.BlockSpec((PREP_TM, V7X_LANES), lambda i: (i, 0)), vec, vec],
        out_specs=[out_spec, out_spec, out_spec],
        out_shape=[out, out, out],
        compiler_params=_params(("parallel",)),
        name="fox_prep",
    )(z, z, z, cum, qg, kg)


def _fox_cum_kernel(f_ref, b_ref, o_ref, carry_ref):
    @pl.when(pl.program_id(0) == 0)
    def _():
        carry_ref[...] = jnp.zeros_like(carry_ref)

    lf = _log_sigmoid(f_ref[...] + b_ref[...])
    cum = jnp.dot(_lower_tri(CUM_TB).astype(F32), lf, preferred_element_type=F32,
                  precision=lax.Precision.HIGHEST) + carry_ref[0:1, :]
    o_ref[...] = cum
    carry_ref[...] = jnp.broadcast_to(cum[CUM_TB - 1:CUM_TB, :], carry_ref.shape)


def _fox_cum(zf, bf):
    m, n = zf.shape
    return pl.pallas_call(
        _fox_cum_kernel,
        grid=(m // CUM_TB,),
        in_specs=[pl.BlockSpec((CUM_TB, n), lambda i: (i, 0)), pl.BlockSpec((1, n), lambda i: (0, 0))],
        out_specs=pl.BlockSpec((CUM_TB, n), lambda i: (i, 0)),
        out_shape=jax.ShapeDtypeStruct((m, n), F32),
        scratch_shapes=[pltpu.VMEM((V7X_SUBLANES, n), F32)],
        compiler_params=_params(("arbitrary",)),
        name="fox_cum",
    )(zf, bf)


def _fox_attn_kernel(q_ref, k_ref, v_ref, og_ref, o_ref, m_ref, acc_ref):
    i = pl.program_id(1)
    q = q_ref[...]
    m_ref[...] = jnp.full_like(m_ref, -jnp.inf)
    acc_ref[...] = jnp.zeros_like(acc_ref)

    def logits(j):
        k0 = pl.multiple_of(j * ATT_TK, ATT_TK)
        return lax.dot_general(q, k_ref[pl.ds(k0, ATT_TK), :], (((1,), (1,)), ((), ())),
                               preferred_element_type=F32)

    def update(s, j):
        k0 = pl.multiple_of(j * ATT_TK, ATT_TK)
        m_prev = m_ref[...]
        m_new = jnp.maximum(m_prev, jnp.max(s, axis=-1, keepdims=True))
        alpha = jnp.exp2(m_prev - m_new)
        p = jnp.exp2(s - jnp.concatenate([m_new] * (ATT_TK // V7X_LANES), axis=1))
        pv = jnp.dot(p.astype(BF16), v_ref[pl.ds(k0, ATT_TK), :], preferred_element_type=F32)
        acc_ref[...] = jnp.concatenate([alpha] * (FOX_DP // V7X_LANES), axis=1) * acc_ref[...] + pv
        m_ref[...] = m_new

    def pair(jj, carry):
        s_a = logits(2 * jj)
        s_b = logits(2 * jj + 1)
        update(s_a, 2 * jj)
        update(s_b, 2 * jj + 1)
        return carry

    lax.fori_loop(0, i, pair, 0)

    row = lax.broadcasted_iota(jnp.int32, (ATT_TQ, ATT_TK), 0)
    col = lax.broadcasted_iota(jnp.int32, (ATT_TQ, ATT_TK), 1)
    update(jnp.where(row >= col, logits(2 * i), -jnp.inf), 2 * i)
    update(jnp.where(row >= col + ATT_TK, logits(2 * i + 1), -jnp.inf), 2 * i + 1)
    acc = acc_ref[...]
    o = acc[:, :FOX_DH] / acc[:, FOX_DH:]
    o_ref[...] = (o * jax.nn.sigmoid(og_ref[...])).astype(BF16)


def _fox_attn(qp, kp, vp, z):
    m = qp.shape[0]
    og_col0 = 3 * FOX_HEADS
    return pl.pallas_call(
        _fox_attn_kernel,
        grid=(FOX_HEADS, m // ATT_TQ),
        in_specs=[
            pl.BlockSpec((ATT_TQ, FOX_DP), lambda h, i: (i, h)),
            pl.BlockSpec((m, FOX_DP), lambda h, i: (0, h)),
            pl.BlockSpec((m, FOX_DP), lambda h, i: (0, h)),
            pl.BlockSpec((ATT_TQ, FOX_DH), lambda h, i: (i, og_col0 + h)),
        ],
        out_specs=pl.BlockSpec((ATT_TQ, FOX_DH), lambda h, i: (i, h)),
        out_shape=jax.ShapeDtypeStruct((m, FOX_D), BF16),
        scratch_shapes=[
            pltpu.VMEM((ATT_TQ, V7X_LANES), F32),
            pltpu.VMEM((ATT_TQ, FOX_DP), F32),
        ],
        compiler_params=_params(("parallel", "arbitrary")),
        name="fox_attn",
    )(qp, kp, vp, z)


def _pad_last(w, n):
    return jnp.pad(w, [(0, 0)] * (w.ndim - 1) + [(0, n - w.shape[-1])])


def _mixer_gla_conv(h, e, layer, p):
    z, zlr = _proj(h, p["mix_norm"], layer, p["ab_w_in"], e, AB_MAIN, p["ab_w_lr"])
    zc, = _proj(h, p["mix_norm"], layer, p["ab_w_conv"], e, 2 * CONV_DIM)
    mix = _gla_conv(z, zc, zlr, p["gla_w_gk2"][e], p["gla_b_gk"][e][None, :], p["gla_out_norm"][e][None, :],
                    p["conv_w"][e], p["conv_b"][e][None, :], p["conv_ln_g"][e][None, :],
                    p["conv_ln_b"][e][None, :])
    return _out_proj(mix, p["ab_w_out"], h, e)


def _mixer_fox(h, o, layer, p):
    z, zf = _proj(h, p["mix_norm"], layer, p["fox_w_in"], o, 4 * FOX_D, p["fox_w_f"])
    cum = _fox_cum(zf, p["fox_b_f"][o][None, :])
    qp, kp, vp = _fox_prep(z, cum, p["fox_q_norm"][o][None, :], p["fox_k_norm"][o][None, :])
    att = _fox_attn(qp, kp, vp, z)
    return _out_proj(att, p["fox_w_out"], h, o)


def kernel(x, ffn1_norm, ffn1_gate, ffn1_up, ffn1_down, mix_norm, ffn2_norm, ffn2_gate, ffn2_up,
           ffn2_down, ab_w_in, gla_w_gk2, gla_b_gk, gla_out_norm, conv_w, conv_b, conv_ln_g,
           conv_ln_b, ab_w_out, fox_w_in, fox_b_f, fox_q_norm, fox_k_norm, fox_w_out):
    b, t, d = x.shape
    assert b == 1, "the time recurrences assume one sequence"
    lr0 = AB_MAIN
    p = dict(
        mix_norm=mix_norm[:, None, :],
        ab_w_in=ab_w_in.astype(BF16),
        ab_w_conv=ab_w_in[:, :, lr0 + GLA_RANK:].astype(BF16),
        ab_w_lr=_pad_last(ab_w_in[:, :, lr0:lr0 + GLA_RANK], V7X_LANES).astype(BF16),
        gla_w_gk2=jnp.pad(gla_w_gk2, ((0, 0), (0, V7X_LANES - GLA_RANK), (0, 0))).astype(BF16),
        gla_b_gk=gla_b_gk, gla_out_norm=gla_out_norm, conv_w=conv_w, conv_b=conv_b,
        conv_ln_g=conv_ln_g, conv_ln_b=conv_ln_b, ab_w_out=ab_w_out.astype(BF16),
        fox_w_in=fox_w_in.astype(BF16),
        fox_w_f=_pad_last(fox_w_in[:, :, 4 * FOX_D:], V7X_LANES).astype(BF16),
        fox_b_f=_pad_last(fox_b_f, V7X_LANES), fox_q_norm=fox_q_norm, fox_k_norm=fox_k_norm,
        fox_w_out=fox_w_out.astype(BF16),
    )
    ffn1 = (ffn1_norm[:, None, :], ffn1_gate.astype(BF16), ffn1_up.astype(BF16), ffn1_down.astype(BF16))
    ffn2 = (ffn2_norm[:, None, :], ffn2_gate.astype(BF16), ffn2_up.astype(BF16), ffn2_down.astype(BF16))
    h = x.reshape(b * t, d)
    for layer in range(DEPTH):
        h = _ffn(h, *ffn1, layer)
        if layer % 2 == 0:
            h = _mixer_gla_conv(h, layer // 2, layer, p)
        else:
            h = _mixer_fox(h, layer // 2, layer, p)
        h = _ffn(h, *ffn2, layer)
    return h.reshape(b, t, d)
```

```python
import functools

import jax
import jax.numpy as jnp
from jax import lax
from jax.experimental import pallas as pl
from jax.experimental.pallas import tpu as pltpu

F32 = jnp.float32
BF16 = jnp.bfloat16

D_MODEL = 2048
SEQ = 8192
DEPTH = 4
D_FF = 5632
GLA_HEADS = 4
GLA_DK = 128
GLA_DV = 256
GLA_QK = GLA_HEADS * GLA_DK
GLA_V = GLA_HEADS * GLA_DV
GLA_RANK = 16
GLA_GATE_TAU = 16.0
GLA_CHUNK = 64
CONV_DIM = D_MODEL // 2
CONV_WIDTH = 31
AB_MAIN = 2 * GLA_QK + 2 * GLA_V
FOX_HEADS = 16
FOX_DH = 128
FOX_D = FOX_HEADS * FOX_DH
EPS = 1e-6

V7X_LANES = 128
V7X_SUBLANES = 8
V7X_VMEM_BYTES = 64 * 1024 * 1024
VMEM_LIMIT = V7X_VMEM_BYTES * 7 // 8

CONV_HALO = 32


def _params(semantics, vmem=VMEM_LIMIT):
    return pltpu.CompilerParams(dimension_semantics=semantics, vmem_limit_bytes=vmem)


def _rms_normed(x, gain):
    ms = jnp.mean(x * x, axis=-1, keepdims=True)
    return x * lax.rsqrt(ms + EPS) * gain


def _log_sigmoid(x):
    return jnp.minimum(x, 0.0) - jnp.log1p(jnp.exp(-jnp.abs(x)))


def _silu(x):
    return x * jax.nn.sigmoid(x)


def _lower_tri(n):
    r = lax.broadcasted_iota(jnp.int32, (n, n), 0)
    c = lax.broadcasted_iota(jnp.int32, (n, n), 1)
    return r >= c


FFN_TM = 1024
FFN_TF = 512
FFN_HEAD_TF = 256


def _ffn_kernel(cast_weights, h_ref, gain_ref, wg_ref, wu_ref, wd_ref, *rest):
    if cast_weights:
        o_ref, wg_out, wu_out, wd_out, xn_ref = rest
    else:
        o_ref, xn_ref = rest[-2:]
    f = pl.program_id(1)

    @pl.when(f == 0)
    def _():
        xn_ref[...] = _rms_normed(h_ref[...], gain_ref[...]).astype(BF16)
        o_ref[...] = jnp.zeros_like(o_ref)

    wg, wu, wd = wg_ref[...], wu_ref[...], wd_ref[...]
    if cast_weights:
        wg, wu, wd = wg.astype(BF16), wu.astype(BF16), wd.astype(BF16)
        wg_out[...] = wg
        wu_out[...] = wu
        wd_out[...] = wd
    x = xn_ref[...]
    g = jnp.dot(x, wg, preferred_element_type=F32)
    u = jnp.dot(x, wu, preferred_element_type=F32)
    a = (_silu(g) * u).astype(BF16)
    o_ref[...] += jnp.dot(a, wd, preferred_element_type=F32)

    @pl.when(f == pl.num_programs(1) - 1)
    def _():
        o_ref[...] = h_ref[...] + 0.5 * o_ref[...]


def _ffn(h, gain, wg, wu, wd, layer):
    m, d = h.shape
    dff = wg.shape[2]
    tfh = FFN_HEAD_TF
    once = pl.Buffered(1)
    out, wg16, wu16, wd16 = pl.pallas_call(
        functools.partial(_ffn_kernel, True),
        grid=(1, dff // tfh),
        in_specs=[
            pl.BlockSpec((FFN_TM, d), lambda i, f: (0, 0), pipeline_mode=once),
            pl.BlockSpec((None, 1, d), lambda i, f: (layer, 0, 0)),
            pl.BlockSpec((None, d, tfh), lambda i, f: (layer, 0, f)),
            pl.BlockSpec((None, d, tfh), lambda i, f: (layer, 0, f)),
            pl.BlockSpec((None, tfh, d), lambda i, f: (layer, f, 0)),
        ],
        out_specs=[
            pl.BlockSpec((FFN_TM, d), lambda i, f: (0, 0), pipeline_mode=once),
            pl.BlockSpec((d, tfh), lambda i, f: (0, f)),
            pl.BlockSpec((d, tfh), lambda i, f: (0, f)),
            pl.BlockSpec((tfh, d), lambda i, f: (f, 0)),
        ],
        out_shape=[
            jax.ShapeDtypeStruct((m, d), F32),
            jax.ShapeDtypeStruct((d, dff), BF16),
            jax.ShapeDtypeStruct((d, dff), BF16),
            jax.ShapeDtypeStruct((dff, d), BF16),
        ],
        scratch_shapes=[pltpu.VMEM((FFN_TM, d), BF16)],
        compiler_params=_params(("arbitrary", "arbitrary")),
        name="ffn_head",
    )(h, gain, wg, wu, wd)
    return pl.pallas_call(
        functools.partial(_ffn_kernel, False),
        grid=(m // FFN_TM - 1, dff // FFN_TF),
        in_specs=[
            pl.BlockSpec((FFN_TM, d), lambda i, f: (i + 1, 0)),
            pl.BlockSpec((None, 1, d), lambda i, f: (layer, 0, 0)),
            pl.BlockSpec((d, FFN_TF), lambda i, f: (0, f)),
            pl.BlockSpec((d, FFN_TF), lambda i, f: (0, f)),
            pl.BlockSpec((FFN_TF, d), lambda i, f: (f, 0)),
            pl.BlockSpec(memory_space=pl.ANY),
        ],
        out_specs=pl.BlockSpec((FFN_TM, d), lambda i, f: (i + 1, 0)),
        out_shape=jax.ShapeDtypeStruct((m, d), F32),
        input_output_aliases={5: 0},
        scratch_shapes=[pltpu.VMEM((FFN_TM, d), BF16)],
        compiler_params=_params(("parallel", "arbitrary")),
        name="ffn",
    )(h, gain, wg16, wu16, wd16, out)


PROJ_TM = 1024
PROJ_TN = 1024


def _proj_kernel(has_side, h_ref, gain_ref, w_ref, *rest):
    if has_side:
        ws_ref, o_ref, os_ref, xn_ref = rest
    else:
        o_ref, xn_ref = rest

    @pl.when(pl.program_id(1) == 0)
    def _():
        xn_ref[...] = _rms_normed(h_ref[...], gain_ref[...]).astype(BF16)
        if has_side:
            os_ref[...] = jnp.dot(xn_ref[...], ws_ref[...], preferred_element_type=F32)

    o_ref[...] = jnp.dot(xn_ref[...], w_ref[...], preferred_element_type=F32)


def _proj(h, gain, layer, w, idx, n, w_side=None):
    m, d = h.shape
    in_specs = [
        pl.BlockSpec((PROJ_TM, d), lambda i, j: (i, 0)),
        pl.BlockSpec((None, 1, d), lambda i, j: (layer, 0, 0)),
        pl.BlockSpec((None, d, PROJ_TN), lambda i, j: (idx, 0, j)),
    ]
    out_specs = [pl.BlockSpec((PROJ_TM, PROJ_TN), lambda i, j: (i, j))]
    out_shape = [jax.ShapeDtypeStruct((m, n), F32)]
    args = [h, gain, w]
    if w_side is not None:
        ns = w_side.shape[2]
        in_specs.append(pl.BlockSpec((None, d, ns), lambda i, j: (idx, 0, 0)))
        out_specs.append(pl.BlockSpec((PROJ_TM, ns), lambda i, j: (i, 0)))
        out_shape.append(jax.ShapeDtypeStruct((m, ns), F32))
        args.append(w_side)
    return pl.pallas_call(
        functools.partial(_proj_kernel, w_side is not None),
        grid=(m // PROJ_TM, n // PROJ_TN),
        in_specs=in_specs,
        out_specs=out_specs,
        out_shape=out_shape,
        scratch_shapes=[pltpu.VMEM((PROJ_TM, d), BF16)],
        compiler_params=_params(("parallel", "arbitrary")),
        name="proj",
    )(*args)


OUT_TM = 512
OUT_TN = 2048


def _out_proj_kernel(a_ref, w_ref, h_ref, o_ref):
    o_ref[...] = h_ref[...] + jnp.dot(a_ref[...], w_ref[...], preferred_element_type=F32)


def _out_proj(a, w, h, idx):
    m, k = a.shape
    n = w.shape[2]
    return pl.pallas_call(
        _out_proj_kernel,
        grid=(m // OUT_TM, n // OUT_TN),
        in_specs=[
            pl.BlockSpec((OUT_TM, k), lambda i, j: (i, 0)),
            pl.BlockSpec((None, k, OUT_TN), lambda i, j: (idx, 0, j)),
            pl.BlockSpec((OUT_TM, OUT_TN), lambda i, j: (i, j)),
        ],
        out_specs=pl.BlockSpec((OUT_TM, OUT_TN), lambda i, j: (i, j)),
        out_shape=jax.ShapeDtypeStruct((m, n), F32),
        compiler_params=_params(("parallel", "parallel")),
        name="out_proj",
    )(a, w, h)


MIX_TT = 512
CONV_RB = 32
CONV_LW = 512


def _gla_conv_kernel(q_ref, k_ref, v_ref, g_ref, cv_ref, cg_ref, lr_ref,
                     wgk_ref, bgk_ref, gnorm_ref, cw_ref, cb_ref, lng_ref, lnb_ref,
                     mix_ref, st_ref, cbuf_ref, la_ref):
    @pl.when(pl.program_id(0) == 0)
    def _():
        st_ref[...] = jnp.zeros_like(st_ref)
        cbuf_ref[0:CONV_HALO, :] = jnp.zeros((CONV_HALO, CONV_DIM), F32)

    gk = jnp.dot(lr_ref[...].astype(BF16), wgk_ref[...], preferred_element_type=F32) + bgk_ref[...]
    la_ref[...] = _log_sigmoid(gk) * (1.0 / GLA_GATE_TAU)

    tri = _lower_tri(GLA_CHUNK)
    tri_f = tri.astype(F32)
    gnorm = gnorm_ref[...]
    mid = GLA_CHUNK // 2

    def chunk(c, carry):
        r0 = pl.multiple_of(c * GLA_CHUNK, GLA_CHUNK)
        rows = pl.ds(r0, GLA_CHUNK)
        cum = jnp.dot(tri_f, la_ref[rows, :], preferred_element_type=F32,
                      precision=lax.Precision.HIGHEST)
        for h in range(GLA_HEADS):
            ks = slice(h * GLA_DK, (h + 1) * GLA_DK)
            vs = slice(h * GLA_DV, (h + 1) * GLA_DV)
            cum_h = cum[:, ks]
            ref = cum_h[mid:mid + 1, :]
            last = cum_h[GLA_CHUNK - 1:GLA_CHUNK, :]
            q = q_ref[rows, ks] * (GLA_DK ** -0.5)
            k = k_ref[rows, ks]
            v = v_ref[rows, vs].astype(BF16)
            qt = (q * jnp.exp(cum_h - ref)).astype(BF16)
            kt = (k * jnp.exp(ref - cum_h)).astype(BF16)
            a = lax.dot_general(qt, kt, (((1,), (1,)), ((), ())), preferred_element_type=F32)
            a = jnp.where(tri, a, 0.0).astype(BF16)
            o = jnp.dot(a, v, preferred_element_type=F32)
            st = st_ref[h]
            q_in = (q * jnp.exp(cum_h)).astype(BF16)
            o = o + lax.dot_general(q_in, st.astype(BF16), (((1,), (1,)), ((), ())),
                                    preferred_element_type=F32)
            k_out = (k * jnp.exp(last - cum_h)).astype(BF16)
            st_ref[h] = st * jnp.exp(last) + lax.dot_general(
                v, k_out, (((0,), (0,)), ((), ())), preferred_element_type=F32)
            on = _rms_normed(o, gnorm)
            mix_ref[rows, vs] = (on * _silu(g_ref[rows, vs])).astype(BF16)
        return carry

    lax.fori_loop(0, MIX_TT // GLA_CHUNK, chunk, 0)

    cbuf_ref[CONV_HALO:CONV_HALO + MIX_TT, :] = cv_ref[...] * jax.nn.sigmoid(cg_ref[...])
    cb = cb_ref[...]
    lng = lng_ref[...]
    lnb = lnb_ref[...]
    tap0 = CONV_HALO - (CONV_WIDTH - 1)

    def conv_rows(rb, carry):
        base = pl.multiple_of(rb * CONV_RB, CONV_RB)
        halves = []
        for lane0 in range(0, CONV_DIM, CONV_LW):
            ls = slice(lane0, lane0 + CONV_LW)
            acc = None
            for s in range(V7X_SUBLANES):
                part = None
                rows = CONV_RB + (V7X_SUBLANES if s else 0)
                for d in range(s, tap0 + CONV_WIDTH, V7X_SUBLANES):
                    if d < tap0:
                        continue
                    win = cbuf_ref[pl.ds(base + (d - s), rows), ls]
                    term = cw_ref[d - tap0:d - tap0 + 1, ls] * win
                    part = term if part is None else part + term
                part = part[s:s + CONV_RB, :]
                acc = part if acc is None else acc + part
            halves.append(acc)
        y = jnp.concatenate(halves, axis=1) + cb
        mu = jnp.mean(y, axis=-1, keepdims=True)
        yc = y - mu
        var = jnp.mean(yc * yc, axis=-1, keepdims=True)
        yn = yc * lax.rsqrt(var + EPS) * lng + lnb
        mix_ref[pl.ds(base, CONV_RB), GLA_V:GLA_V + CONV_DIM] = _silu(yn).astype(BF16)
        return carry

    lax.fori_loop(0, MIX_TT // CONV_RB, conv_rows, 0)
    cbuf_ref[0:CONV_HALO, :] = cbuf_ref[MIX_TT:MIX_TT + CONV_HALO, :]


def _gla_conv(z, zc, zlr, wgk, bgk, gnorm, cw, cb, lng, lnb):
    m = z.shape[0]
    row = lambda shape: pl.BlockSpec(shape, lambda t: (0, 0))
    return pl.pallas_call(
        _gla_conv_kernel,
        grid=(m // MIX_TT,),
        in_specs=[
            pl.BlockSpec((MIX_TT, GLA_QK), lambda t: (t, 0)),
            pl.BlockSpec((MIX_TT, GLA_QK), lambda t: (t, 1)),
            pl.BlockSpec((MIX_TT, GLA_V), lambda t: (t, 1)),
            pl.BlockSpec((MIX_TT, GLA_V), lambda t: (t, 2)),
            pl.BlockSpec((MIX_TT, CONV_DIM), lambda t: (t, 0)),
            pl.BlockSpec((MIX_TT, CONV_DIM), lambda t: (t, 1)),
            pl.BlockSpec((MIX_TT, V7X_LANES), lambda t: (t, 0)),
            row((V7X_LANES, GLA_QK)),
            row((1, GLA_QK)),
            row((1, GLA_DV)),
            row((CONV_WIDTH, CONV_DIM)),
            row((1, CONV_DIM)),
            row((1, CONV_DIM)),
            row((1, CONV_DIM)),
        ],
        out_specs=pl.BlockSpec((MIX_TT, GLA_V + CONV_DIM), lambda t: (t, 0)),
        out_shape=jax.ShapeDtypeStruct((m, GLA_V + CONV_DIM), BF16),
        scratch_shapes=[
            pltpu.VMEM((GLA_HEADS, GLA_DV, GLA_DK), F32),
            pltpu.VMEM((CONV_HALO + MIX_TT, CONV_DIM), F32),
            pltpu.VMEM((MIX_TT, GLA_QK), F32),
        ],
        compiler_params=_params(("arbitrary",)),
        name="gla_conv",
    )(z, z, z, z, zc, zc, zlr, wgk, bgk, gnorm, cw, cb, lng, lnb)


PREP_TM = 256
CUM_TB = 256
ATT_TQ = 1024
ATT_TK = 512
FOX_DP = 2 * FOX_DH
LOG2E = 1.4426950408889634


def _split3(c):
    hi = c.astype(BF16).astype(F32)
    r = c - hi
    mid = r.astype(BF16).astype(F32)
    lo = (r - mid).astype(BF16).astype(F32)
    return hi, mid, lo


def _fox_prep_kernel(zq_ref, zk_ref, zv_ref, cum_ref, qg_ref, kg_ref, qp_ref, kp_ref, vp_ref):
    qg = qg_ref[...] * (FOX_DH ** -0.5 * LOG2E)
    kg = kg_ref[...]
    lane = lax.broadcasted_iota(jnp.int32, (PREP_TM, V7X_LANES), 1)
    ones = jnp.ones((PREP_TM, V7X_LANES), BF16)
    cum = cum_ref[...] * LOG2E
    for h in range(FOX_HEADS):
        hs = slice(h * FOX_DH, (h + 1) * FOX_DH)
        f0 = h * FOX_DP
        hi, mid, lo = _split3(jnp.broadcast_to(cum[:, h:h + 1], (PREP_TM, V7X_LANES)))
        qb = jnp.where(lane == 0, hi, jnp.where(lane == 1, mid, jnp.where(lane == 2, lo,
             jnp.where(lane < 6, 1.0, 0.0))))
        kb = jnp.where(lane < 3, 1.0, jnp.where(lane == 3, -hi, jnp.where(lane == 4, -mid,
             jnp.where(lane == 5, -lo, 0.0))))
        qp_ref[:, f0:f0 + FOX_DH] = _rms_normed(zq_ref[:, hs], qg).astype(BF16)
        qp_ref[:, f0 + FOX_DH:f0 + FOX_DP] = qb.astype(BF16)
        kp_ref[:, f0:f0 + FOX_DH] = _rms_normed(zk_ref[:, hs], kg).astype(BF16)
        kp_ref[:, f0 + FOX_DH:f0 + FOX_DP] = kb.astype(BF16)
        vp_ref[:, f0:f0 + FOX_DH] = zv_ref[:, hs].astype(BF16)
        vp_ref[:, f0 + FOX_DH:f0 + FOX_DP] = ones


def _fox_prep(z, cum, qg, kg):
    m = z.shape[0]
    blk = lambda c: pl.BlockSpec((PREP_TM, FOX_D), lambda i, c=c: (i, c))
    vec = pl.BlockSpec((1, FOX_DH), lambda i: (0, 0))
    out_spec = pl.BlockSpec((PREP_TM, FOX_HEADS * FOX_DP), lambda i: (i, 0))
    out = jax.ShapeDtypeStruct((m, FOX_HEADS * FOX_DP), BF16)
    return pl.pallas_call(
        _fox_prep_kernel,
        grid=(m // PREP_TM,),
        in_specs=[blk(0), blk(1), blk(2), pl.BlockSpec((PREP_TM, V7X_LANES), lambda i: (i, 0)), vec, vec],
        out_specs=[out_spec, out_spec, out_spec],
        out_shape=[out, out, out],
        compiler_params=_params(("parallel",)),
        name="fox_prep",
    )(z, z, z, cum, qg, kg)


def _fox_cum_kernel(f_ref, b_ref, o_ref, carry_ref):
    @pl.when(pl.program_id(0) == 0)
    def _():
        carry_ref[...] = jnp.zeros_like(carry_ref)

    lf = _log_sigmoid(f_ref[...] + b_ref[...])
    cum = jnp.dot(_lower_tri(CUM_TB).astype(F32), lf, preferred_element_type=F32,
                  precision=lax.Precision.HIGHEST) + carry_ref[0:1, :]
    o_ref[...] = cum
    carry_ref[...] = jnp.broadcast_to(cum[CUM_TB - 1:CUM_TB, :], carry_ref.shape)


def _fox_cum(zf, bf):
    m, n = zf.shape
    return pl.pallas_call(
        _fox_cum_kernel,
        grid=(m // CUM_TB,),
        in_specs=[pl.BlockSpec((CUM_TB, n), lambda i: (i, 0)), pl.BlockSpec((1, n), lambda i: (0, 0))],
        out_specs=pl.BlockSpec((CUM_TB, n), lambda i: (i, 0)),
        out_shape=jax.ShapeDtypeStruct((m, n), F32),
        scratch_shapes=[pltpu.VMEM((V7X_SUBLANES, n), F32)],
        compiler_params=_params(("arbitrary",)),
        name="fox_cum",
    )(zf, bf)


def _fox_attn_kernel(q_ref, k_ref, v_ref, og_ref, o_ref, m_ref, acc_ref):
    i = pl.program_id(1)
    q = q_ref[...]
    m_ref[...] = jnp.full_like(m_ref, -jnp.inf)
    acc_ref[...] = jnp.zeros_like(acc_ref)

    def logits(j):
        k0 = pl.multiple_of(j * ATT_TK, ATT_TK)
        return lax.dot_general(q, k_ref[pl.ds(k0, ATT_TK), :], (((1,), (1,)), ((), ())),
                               preferred_element_type=F32)

    def update(s, j):
        k0 = pl.multiple_of(j * ATT_TK, ATT_TK)
        m_prev = m_ref[...]
        m_new = jnp.maximum(m_prev, jnp.max(s, axis=-1, keepdims=True))
        alpha = jnp.exp2(m_prev - m_new)
        p = jnp.exp2(s - jnp.concatenate([m_new] * (ATT_TK // V7X_LANES), axis=1))
        pv = jnp.dot(p.astype(BF16), v_ref[pl.ds(k0, ATT_TK), :], preferred_element_type=F32)
        acc_ref[...] = jnp.concatenate([alpha] * (FOX_DP // V7X_LANES), axis=1) * acc_ref[...] + pv
        m_ref[...] = m_new

    def pair(jj, carry):
        s_a = logits(2 * jj)
        s_b = logits(2 * jj + 1)
        update(s_a, 2 * jj)
        update(s_b, 2 * jj + 1)
        return carry

    lax.fori_loop(0, i, pair, 0)

    row = lax.broadcasted_iota(jnp.int32, (ATT_TQ, ATT_TK), 0)
    col = lax.broadcasted_iota(jnp.int32, (ATT_TQ, ATT_TK), 1)
    update(jnp.where(row >= col, logits(2 * i), -jnp.inf), 2 * i)
    update(jnp.where(row >= col + ATT_TK, logits(2 * i + 1), -jnp.inf), 2 * i + 1)
    acc = acc_ref[...]
    o = acc[:, :FOX_DH] / acc[:, FOX_DH:]
    o_ref[...] = (o * jax.nn.sigmoid(og_ref[...])).astype(BF16)


def _fox_attn(qp, kp, vp, z):
    m = qp.shape[0]
    og_col0 = 3 * FOX_HEADS
    return pl.pallas_call(
        _fox_attn_kernel,
        grid=(FOX_HEADS, m // ATT_TQ),
        in_specs=[
            pl.BlockSpec((ATT_TQ, FOX_DP), lambda h, i: (i, h)),
            pl.BlockSpec((m, FOX_DP), lambda h, i: (0, h)),
            pl.BlockSpec((m, FOX_DP), lambda h, i: (0, h)),
            pl.BlockSpec((ATT_TQ, FOX_DH), lambda h, i: (i, og_col0 + h)),
        ],
        out_specs=pl.BlockSpec((ATT_TQ, FOX_DH), lambda h, i: (i, h)),
        out_shape=jax.ShapeDtypeStruct((m, FOX_D), BF16),
        scratch_shapes=[
            pltpu.VMEM((ATT_TQ, V7X_LANES), F32),
            pltpu.VMEM((ATT_TQ, FOX_DP), F32),
        ],
        compiler_params=_params(("parallel", "arbitrary")),
        name="fox_attn",
    )(qp, kp, vp, z)


def _pad_last(w, n):
    return jnp.pad(w, [(0, 0)] * (w.ndim - 1) + [(0, n - w.shape[-1])])


def _mixer_gla_conv(h, e, layer, p):
    z, zlr = _proj(h, p["mix_norm"], layer, p["ab_w_in"], e, AB_MAIN, p["ab_w_lr"])
    zc, = _proj(h, p["mix_norm"], layer, p["ab_w_conv"], e, 2 * CONV_DIM)
    mix = _gla_conv(z, zc, zlr, p["gla_w_gk2"][e], p["gla_b_gk"][e][None, :], p["gla_out_norm"][e][None, :],
                    p["conv_w"][e], p["conv_b"][e][None, :], p["conv_ln_g"][e][None, :],
                    p["conv_ln_b"][e][None, :])
    return _out_proj(mix, p["ab_w_out"], h, e)


def _mixer_fox(h, o, layer, p):
    z, zf = _proj(h, p["mix_norm"], layer, p["fox_w_in"], o, 4 * FOX_D, p["fox_w_f"])
    cum = _fox_cum(zf, p["fox_b_f"][o][None, :])
    qp, kp, vp = _fox_prep(z, cum, p["fox_q_norm"][o][None, :], p["fox_k_norm"][o][None, :])
    att = _fox_attn(qp, kp, vp, z)
    return _out_proj(att, p["fox_w_out"], h, o)


def kernel(x, ffn1_norm, ffn1_gate, ffn1_up, ffn1_down, mix_norm, ffn2_norm, ffn2_gate, ffn2_up,
           ffn2_down, ab_w_in, gla_w_gk2, gla_b_gk, gla_out_norm, conv_w, conv_b, conv_ln_g,
           conv_ln_b, ab_w_out, fox_w_in, fox_b_f, fox_q_norm, fox_k_norm, fox_w_out):
    b, t, d = x.shape
    assert b == 1, "the time recurrences assume one sequence"
    lr0 = AB_MAIN
    p = dict(
        mix_norm=mix_norm[:, None, :],
        ab_w_in=ab_w_in.astype(BF16),
        ab_w_conv=ab_w_in[:, :, lr0 + GLA_RANK:].astype(BF16),
        ab_w_lr=_pad_last(ab_w_in[:, :, lr0:lr0 + GLA_RANK], V7X_LANES).astype(BF16),
        gla_w_gk2=jnp.pad(gla_w_gk2, ((0, 0), (0, V7X_LANES - GLA_RANK), (0, 0))).astype(BF16),
        gla_b_gk=gla_b_gk, gla_out_norm=gla_out_norm, conv_w=conv_w, conv_b=conv_b,
        conv_ln_g=conv_ln_g, conv_ln_b=conv_ln_b, ab_w_out=ab_w_out.astype(BF16),
        fox_w_in=fox_w_in.astype(BF16),
        fox_w_f=_pad_last(fox_w_in[:, :, 4 * FOX_D:], V7X_LANES).astype(BF16),
        fox_b_f=_pad_last(fox_b_f, V7X_LANES), fox_q_norm=fox_q_norm, fox_k_norm=fox_k_norm,
        fox_w_out=fox_w_out.astype(BF16),
    )
    ffn1 = (ffn1_norm[:, None, :], ffn1_gate, ffn1_up, ffn1_down)
    ffn2 = (ffn2_norm[:, None, :], ffn2_gate, ffn2_up, ffn2_down)
    h = x.reshape(b * t, d)
    for layer in range(DEPTH):
        h = _ffn(h, *ffn1, layer)
        if layer % 2 == 0:
            h = _mixer_gla_conv(h, layer // 2, layer, p)
        else:
            h = _mixer_fox(h, layer // 2, layer, p)
        h = _ffn(h, *ffn2, layer)
    return h.reshape(b, t, d)
```

```python
import functools

import jax
import jax.numpy as jnp
from jax import lax
from jax.experimental import pallas as pl
from jax.experimental.pallas import tpu as pltpu

F32 = jnp.float32
BF16 = jnp.bfloat16

D_MODEL = 2048
SEQ = 8192
DEPTH = 4
D_FF = 5632
GLA_HEADS = 4
GLA_DK = 128
GLA_DV = 256
GLA_QK = GLA_HEADS * GLA_DK
GLA_V = GLA_HEADS * GLA_DV
GLA_RANK = 16
GLA_GATE_TAU = 16.0
GLA_CHUNK = 64
CONV_DIM = D_MODEL // 2
CONV_WIDTH = 31
AB_MAIN = 2 * GLA_QK + 2 * GLA_V
FOX_HEADS = 16
FOX_DH = 128
FOX_D = FOX_HEADS * FOX_DH
EPS = 1e-6

V7X_LANES = 128
V7X_SUBLANES = 8
V7X_VMEM_BYTES = 64 * 1024 * 1024
VMEM_LIMIT = V7X_VMEM_BYTES * 7 // 8

CONV_HALO = 32


def _params(semantics, vmem=VMEM_LIMIT):
    return pltpu.CompilerParams(dimension_semantics=semantics, vmem_limit_bytes=vmem)


def _rms_normed(x, gain):
    ms = jnp.mean(x * x, axis=-1, keepdims=True)
    return x * lax.rsqrt(ms + EPS) * gain


def _log_sigmoid(x):
    return jnp.minimum(x, 0.0) - jnp.log1p(jnp.exp(-jnp.abs(x)))


def _silu(x):
    return x * jax.nn.sigmoid(x)


def _lower_tri(n):
    r = lax.broadcasted_iota(jnp.int32, (n, n), 0)
    c = lax.broadcasted_iota(jnp.int32, (n, n), 1)
    return r >= c


FFN_TM = 1024
FFN_TF = 512
FFN_HEAD_TF = 256


def _ffn_kernel(cast_weights, h_ref, gain_ref, wg_ref, wu_ref, wd_ref, *rest):
    if cast_weights:
        o_ref, wg_out, wu_out, wd_out, xn_ref = rest
    else:
        o_ref, xn_ref = rest[-2:]
    f = pl.program_id(1)

    @pl.when(f == 0)
    def _():
        xn_ref[...] = _rms_normed(h_ref[...], gain_ref[...]).astype(BF16)
        o_ref[...] = jnp.zeros_like(o_ref)

    wg, wu, wd = wg_ref[...], wu_ref[...], wd_ref[...]
    if cast_weights:
        wg, wu, wd = wg.astype(BF16), wu.astype(BF16), wd.astype(BF16)
        wg_out[...] = wg
        wu_out[...] = wu
        wd_out[...] = wd
    x = xn_ref[...]
    g = jnp.dot(x, wg, preferred_element_type=F32)
    u = jnp.dot(x, wu, preferred_element_type=F32)
    a = (_silu(g) * u).astype(BF16)
    o_ref[...] += jnp.dot(a, wd, preferred_element_type=F32)

    @pl.when(f == pl.num_programs(1) - 1)
    def _():
        o_ref[...] = h_ref[...] + 0.5 * o_ref[...]


def _ffn(h, gain, wg, wu, wd, layer):
    m, d = h.shape
    dff = wg.shape[2]
    tfh = FFN_HEAD_TF
    once = pl.Buffered(1)
    out, wg16, wu16, wd16 = pl.pallas_call(
        functools.partial(_ffn_kernel, True),
        grid=(1, dff // tfh),
        in_specs=[
            pl.BlockSpec((FFN_TM, d), lambda i, f: (0, 0), pipeline_mode=once),
            pl.BlockSpec((None, 1, d), lambda i, f: (layer, 0, 0)),
            pl.BlockSpec((None, d, tfh), lambda i, f: (layer, 0, f)),
            pl.BlockSpec((None, d, tfh), lambda i, f: (layer, 0, f)),
            pl.BlockSpec((None, tfh, d), lambda i, f: (layer, f, 0)),
        ],
        out_specs=[
            pl.BlockSpec((FFN_TM, d), lambda i, f: (0, 0), pipeline_mode=once),
            pl.BlockSpec((d, tfh), lambda i, f: (0, f)),
            pl.BlockSpec((d, tfh), lambda i, f: (0, f)),
            pl.BlockSpec((tfh, d), lambda i, f: (f, 0)),
        ],
        out_shape=[
            jax.ShapeDtypeStruct((m, d), F32),
            jax.ShapeDtypeStruct((d, dff), BF16),
            jax.ShapeDtypeStruct((d, dff), BF16),
            jax.ShapeDtypeStruct((dff, d), BF16),
        ],
        scratch_shapes=[pltpu.VMEM((FFN_TM, d), BF16)],
        compiler_params=_params(("arbitrary", "arbitrary")),
        name="ffn_head",
    )(h, gain, wg, wu, wd)
    return pl.pallas_call(
        functools.partial(_ffn_kernel, False),
        grid=(m // FFN_TM - 1, dff // FFN_TF),
        in_specs=[
            pl.BlockSpec((FFN_TM, d), lambda i, f: (i + 1, 0)),
            pl.BlockSpec((None, 1, d), lambda i, f: (layer, 0, 0)),
            pl.BlockSpec((d, FFN_TF), lambda i, f: (0, f)),
            pl.BlockSpec((d, FFN_TF), lambda i, f: (0, f)),
            pl.BlockSpec((FFN_TF, d), lambda i, f: (f, 0)),
            pl.BlockSpec(memory_space=pl.ANY),
        ],
        out_specs=pl.BlockSpec((FFN_TM, d), lambda i, f: (i + 1, 0)),
        out_shape=jax.ShapeDtypeStruct((m, d), F32),
        input_output_aliases={5: 0},
        scratch_shapes=[pltpu.VMEM((FFN_TM, d), BF16)],
        compiler_params=_params(("parallel", "arbitrary")),
        name="ffn",
    )(h, gain, wg16, wu16, wd16, out)


PROJ_TM = 1024
PROJ_TN = 1024


def _proj_kernel(has_side, cast_weights, n_prev, h_ref, gain_ref, w_ref, *rest):
    rest = list(rest)
    ws_ref = rest.pop(0) if has_side else None
    del rest[:n_prev]
    o_ref = rest.pop(0)
    os_ref = rest.pop(0) if has_side else None
    w_out = rest.pop(0) if cast_weights else None
    (xn_ref,) = rest

    @pl.when(pl.program_id(1) == 0)
    def _():
        xn_ref[...] = _rms_normed(h_ref[...], gain_ref[...]).astype(BF16)
        if has_side:
            os_ref[...] = jnp.dot(xn_ref[...], ws_ref[...], preferred_element_type=F32)

    w = w_ref[...]
    if cast_weights:
        w = w.astype(BF16)
        w_out[...] = w
    o_ref[...] = jnp.dot(xn_ref[...], w, preferred_element_type=F32)


def _proj_call(h, gain, layer, w, w_spec, n, side, row0, rows, cast_weights, prev):
    m, d = h.shape
    has_side = side is not None
    in_specs = [
        pl.BlockSpec((PROJ_TM, d), lambda i, j: (i + row0, 0),
                     pipeline_mode=pl.Buffered(1) if rows == PROJ_TM else None),
        pl.BlockSpec((None, 1, d), lambda i, j: (layer, 0, 0)),
        w_spec,
    ]
    out_specs = [pl.BlockSpec((PROJ_TM, PROJ_TN), lambda i, j: (i + row0, j))]
    out_shape = [jax.ShapeDtypeStruct((m, n), F32)]
    args = [h, gain, w]
    if has_side:
        w_side, idx = side
        ns = w_side.shape[2]
        in_specs.append(pl.BlockSpec((None, d, ns), lambda i, j: (idx, 0, 0)))
        out_specs.append(pl.BlockSpec((PROJ_TM, ns), lambda i, j: (i + row0, 0)))
        out_shape.append(jax.ShapeDtypeStruct((m, ns), F32))
        args.append(w_side)
    aliases = {}
    for k, buf in enumerate(prev):
        aliases[len(args)] = k
        in_specs.append(pl.BlockSpec(memory_space=pl.ANY))
        args.append(buf)
    if cast_weights:
        out_specs.append(pl.BlockSpec((d, PROJ_TN), lambda i, j: (0, j)))
        out_shape.append(jax.ShapeDtypeStruct((d, n), BF16))
    return pl.pallas_call(
        functools.partial(_proj_kernel, has_side, cast_weights, len(prev)),
        grid=(rows // PROJ_TM, n // PROJ_TN),
        in_specs=in_specs,
        out_specs=out_specs,
        out_shape=out_shape,
        input_output_aliases=aliases,
        scratch_shapes=[pltpu.VMEM((PROJ_TM, d), BF16)],
        compiler_params=_params(("arbitrary", "arbitrary")),
        name="proj_head" if cast_weights else "proj",
    )(*args)


def _proj(h, gain, layer, w, idx, n, w_side=None):
    m, d = h.shape
    side = None if w_side is None else (w_side, idx)
    stacked = pl.BlockSpec((None, d, PROJ_TN), lambda i, j: (idx, 0, j))
    if w.dtype == BF16:
        return _proj_call(h, gain, layer, w, stacked, n, side, 0, m, False, [])
    *head, w16 = _proj_call(h, gain, layer, w, stacked, n, side, 0, PROJ_TM, True, [])
    flat = pl.BlockSpec((d, PROJ_TN), lambda i, j: (0, j))
    return _proj_call(h, gain, layer, w16, flat, n, side, 1, m - PROJ_TM, False, head)


OUT_TM = 512
OUT_TN = 2048


def _out_proj_kernel(a_ref, w_ref, h_ref, o_ref):
    o_ref[...] = h_ref[...] + jnp.dot(a_ref[...], w_ref[...], preferred_element_type=F32)


def _out_proj(a, w, h, idx):
    m, k = a.shape
    n = w.shape[2]
    return pl.pallas_call(
        _out_proj_kernel,
        grid=(m // OUT_TM, n // OUT_TN),
        in_specs=[
            pl.BlockSpec((OUT_TM, k), lambda i, j: (i, 0)),
            pl.BlockSpec((None, k, OUT_TN), lambda i, j: (idx, 0, j)),
            pl.BlockSpec((OUT_TM, OUT_TN), lambda i, j: (i, j)),
        ],
        out_specs=pl.BlockSpec((OUT_TM, OUT_TN), lambda i, j: (i, j)),
        out_shape=jax.ShapeDtypeStruct((m, n), F32),
        compiler_params=_params(("parallel", "parallel")),
        name="out_proj",
    )(a, w, h)


MIX_TT = 512
CONV_RB = 128
CONV_LW = 128


def _gla_conv_kernel(q_ref, k_ref, v_ref, g_ref, cv_ref, cg_ref, lr_ref,
                     wgk_ref, bgk_ref, gnorm_ref, cw_ref, cb_ref, lng_ref, lnb_ref,
                     mix_ref, st_ref, cbuf_ref, la_ref):
    @pl.when(pl.program_id(0) == 0)
    def _():
        st_ref[...] = jnp.zeros_like(st_ref)
        cbuf_ref[0:CONV_HALO, :] = jnp.zeros((CONV_HALO, CONV_DIM), F32)

    gk = jnp.dot(lr_ref[...].astype(BF16), wgk_ref[...], preferred_element_type=F32) + bgk_ref[...]
    la_ref[...] = _log_sigmoid(gk) * (1.0 / GLA_GATE_TAU)

    tri = _lower_tri(GLA_CHUNK)
    tri_f = tri.astype(F32)
    gnorm = gnorm_ref[...]
    mid = GLA_CHUNK // 2

    def chunk(c, carry):
        r0 = pl.multiple_of(c * GLA_CHUNK, GLA_CHUNK)
        rows = pl.ds(r0, GLA_CHUNK)
        cum = jnp.dot(tri_f, la_ref[rows, :], preferred_element_type=F32,
                      precision=lax.Precision.HIGHEST)
        for h in range(GLA_HEADS):
            ks = slice(h * GLA_DK, (h + 1) * GLA_DK)
            vs = slice(h * GLA_DV, (h + 1) * GLA_DV)
            cum_h = cum[:, ks]
            ref = cum_h[mid:mid + 1, :]
            last = cum_h[GLA_CHUNK - 1:GLA_CHUNK, :]
            q = q_ref[rows, ks] * (GLA_DK ** -0.5)
            k = k_ref[rows, ks]
            v = v_ref[rows, vs].astype(BF16)
            qt = (q * jnp.exp(cum_h - ref)).astype(BF16)
            kt = (k * jnp.exp(ref - cum_h)).astype(BF16)
            a = lax.dot_general(qt, kt, (((1,), (1,)), ((), ())), preferred_element_type=F32)
            a = jnp.where(tri, a, 0.0).astype(BF16)
            o = jnp.dot(a, v, preferred_element_type=F32)
            st = st_ref[h]
            q_in = (q * jnp.exp(cum_h)).astype(BF16)
            o = o + lax.dot_general(q_in, st.astype(BF16), (((1,), (1,)), ((), ())),
                                    preferred_element_type=F32)
            k_out = (k * jnp.exp(last - cum_h)).astype(BF16)
            st_ref[h] = st * jnp.exp(last) + lax.dot_general(
                v, k_out, (((0,), (0,)), ((), ())), preferred_element_type=F32)
            on = _rms_normed(o, gnorm)
            mix_ref[rows, vs] = (on * _silu(g_ref[rows, vs])).astype(BF16)
        return carry

    lax.fori_loop(0, MIX_TT // GLA_CHUNK, chunk, 0, unroll=4)

    cbuf_ref[CONV_HALO:CONV_HALO + MIX_TT, :] = cv_ref[...] * jax.nn.sigmoid(cg_ref[...])
    cb = cb_ref[...]
    lng = lng_ref[...]
    lnb = lnb_ref[...]
    tap0 = CONV_HALO - (CONV_WIDTH - 1)

    def conv_rows(rb, carry):
        base = pl.multiple_of(rb * CONV_RB, CONV_RB)
        halves = []
        for lane0 in range(0, CONV_DIM, CONV_LW):
            ls = slice(lane0, lane0 + CONV_LW)
            acc = None
            for s in range(V7X_SUBLANES):
                part = None
                rows = CONV_RB + (V7X_SUBLANES if s else 0)
                for d in range(s, tap0 + CONV_WIDTH, V7X_SUBLANES):
                    if d < tap0:
                        continue
                    win = cbuf_ref[pl.ds(base + (d - s), rows), ls]
                    term = cw_ref[d - tap0:d - tap0 + 1, ls] * win
                    part = term if part is None else part + term
                part = part[s:s + CONV_RB, :]
                acc = part if acc is None else acc + part
            halves.append(acc)
        y = jnp.concatenate(halves, axis=1) + cb
        mu = jnp.mean(y, axis=-1, keepdims=True)
        yc = y - mu
        var = jnp.mean(yc * yc, axis=-1, keepdims=True)
        yn = yc * lax.rsqrt(var + EPS) * lng + lnb
        mix_ref[pl.ds(base, CONV_RB), GLA_V:GLA_V + CONV_DIM] = _silu(yn).astype(BF16)
        return carry

    lax.fori_loop(0, MIX_TT // CONV_RB, conv_rows, 0)
    cbuf_ref[0:CONV_HALO, :] = cbuf_ref[MIX_TT:MIX_TT + CONV_HALO, :]


def _gla_conv(z, zc, zlr, wgk, bgk, gnorm, cw, cb, lng, lnb):
    m = z.shape[0]
    row = lambda shape: pl.BlockSpec(shape, lambda t: (0, 0))
    return pl.pallas_call(
        _gla_conv_kernel,
        grid=(m // MIX_TT,),
        in_specs=[
            pl.BlockSpec((MIX_TT, GLA_QK), lambda t: (t, 0)),
            pl.BlockSpec((MIX_TT, GLA_QK), lambda t: (t, 1)),
            pl.BlockSpec((MIX_TT, GLA_V), lambda t: (t, 1)),
            pl.BlockSpec((MIX_TT, GLA_V), lambda t: (t, 2)),
            pl.BlockSpec((MIX_TT, CONV_DIM), lambda t: (t, 0)),
            pl.BlockSpec((MIX_TT, CONV_DIM), lambda t: (t, 1)),
            pl.BlockSpec((MIX_TT, V7X_LANES), lambda t: (t, 0)),
            row((V7X_LANES, GLA_QK)),
            row((1, GLA_QK)),
            row((1, GLA_DV)),
            row((CONV_WIDTH, CONV_DIM)),
            row((1, CONV_DIM)),
            row((1, CONV_DIM)),
            row((1, CONV_DIM)),
        ],
        out_specs=pl.BlockSpec((MIX_TT, GLA_V + CONV_DIM), lambda t: (t, 0)),
        out_shape=jax.ShapeDtypeStruct((m, GLA_V + CONV_DIM), BF16),
        scratch_shapes=[
            pltpu.VMEM((GLA_HEADS, GLA_DV, GLA_DK), F32),
            pltpu.VMEM((CONV_HALO + MIX_TT, CONV_DIM), F32),
            pltpu.VMEM((MIX_TT, GLA_QK), F32),
        ],
        compiler_params=_params(("arbitrary",)),
        name="gla_conv",
    )(z, z, z, z, zc, zc, zlr, wgk, bgk, gnorm, cw, cb, lng, lnb)


PREP_TM = 256
CUM_TB = 256
ATT_TQ = 1024
ATT_TK = 512
FOX_DP = 2 * FOX_DH
LOG2E = 1.4426950408889634


def _split3(c):
    hi = c.astype(BF16).astype(F32)
    r = c - hi
    mid = r.astype(BF16).astype(F32)
    lo = (r - mid).astype(BF16).astype(F32)
    return hi, mid, lo


def _fox_prep_kernel(zq_ref, zk_ref, zv_ref, cum_ref, qg_ref, kg_ref, qp_ref, kp_ref, vp_ref):
    qg = qg_ref[...] * (FOX_DH ** -0.5 * LOG2E)
    kg = kg_ref[...]
    lane = lax.broadcasted_iota(jnp.int32, (PREP_TM, V7X_LANES), 1)
    ones = jnp.ones((PREP_TM, V7X_LANES), BF16)
    cum = cum_ref[...] * LOG2E
    for h in range(FOX_HEADS):
        hs = slice(h * FOX_DH, (h + 1) * FOX_DH)
        f0 = h * FOX_DP
        hi, mid, lo = _split3(jnp.broadcast_to(cum[:, h:h + 1], (PREP_TM, V7X_LANES)))
        qb = jnp.where(lane == 0, hi, jnp.where(lane == 1, mid, jnp.where(lane == 2, lo,
             jnp.where(lane < 6, 1.0, 0.0))))
        kb = jnp.where(lane < 3, 1.0, jnp.where(lane == 3, -hi, jnp.where(lane == 4, -mid,
             jnp.where(lane == 5, -lo, 0.0))))
        qp_ref[:, f0:f0 + FOX_DH] = _rms_normed(zq_ref[:, hs], qg).astype(BF16)
        qp_ref[:, f0 + FOX_DH:f0 + FOX_DP] = qb.astype(BF16)
        kp_ref[:, f0:f0 + FOX_DH] = _rms_normed(zk_ref[:, hs], kg).astype(BF16)
        kp_ref[:, f0 + FOX_DH:f0 + FOX_DP] = kb.astype(BF16)
        vp_ref[:, f0:f0 + FOX_DH] = zv_ref[:, hs].astype(BF16)
        vp_ref[:, f0 + FOX_DH:f0 + FOX_DP] = ones


def _fox_prep(z, cum, qg, kg):
    m = z.shape[0]
    blk = lambda c: pl.BlockSpec((PREP_TM, FOX_D), lambda i, c=c: (i, c))
    vec = pl.BlockSpec((1, FOX_DH), lambda i: (0, 0))
    out_spec = pl.BlockSpec((PREP_TM, FOX_HEADS * FOX_DP), lambda i: (i, 0))
    out = jax.ShapeDtypeStruct((m, FOX_HEADS * FOX_DP), BF16)
    return pl.pallas_call(
        _fox_prep_kernel,
        grid=(m // PREP_TM,),
        in_specs=[blk(0), blk(1), blk(2), pl.BlockSpec((PREP_TM, V7X_LANES), lambda i: (i, 0)), vec, vec],
        out_specs=[out_spec, out_spec, out_spec],
        out_shape=[out, out, out],
        compiler_params=_params(("parallel",)),
        name="fox_prep",
    )(z, z, z, cum, qg, kg)


def _fox_cum_kernel(f_ref, b_ref, o_ref, carry_ref):
    @pl.when(pl.program_id(0) == 0)
    def _():
        carry_ref[...] = jnp.zeros_like(carry_ref)

    lf = _log_sigmoid(f_ref[...] + b_ref[...])
    cum = jnp.dot(_lower_tri(CUM_TB).astype(F32), lf, preferred_element_type=F32,
                  precision=lax.Precision.HIGHEST) + carry_ref[0:1, :]
    o_ref[...] = cum
    carry_ref[...] = jnp.broadcast_to(cum[CUM_TB - 1:CUM_TB, :], carry_ref.shape)


def _fox_cum(zf, bf):
    m, n = zf.shape
    return pl.pallas_call(
        _fox_cum_kernel,
        grid=(m // CUM_TB,),
        in_specs=[pl.BlockSpec((CUM_TB, n), lambda i: (i, 0)), pl.BlockSpec((1, n), lambda i: (0, 0))],
        out_specs=pl.BlockSpec((CUM_TB, n), lambda i: (i, 0)),
        out_shape=jax.ShapeDtypeStruct((m, n), F32),
        scratch_shapes=[pltpu.VMEM((V7X_SUBLANES, n), F32)],
        compiler_params=_params(("arbitrary",)),
        name="fox_cum",
    )(zf, bf)


def _fox_attn_kernel(q_ref, k_ref, v_ref, og_ref, o_ref, m_ref, acc_ref):
    i = pl.program_id(1)
    q = q_ref[...]
    m_ref[...] = jnp.full_like(m_ref, -jnp.inf)
    acc_ref[...] = jnp.zeros_like(acc_ref)

    def logits(j):
        k0 = pl.multiple_of(j * ATT_TK, ATT_TK)
        return lax.dot_general(q, k_ref[pl.ds(k0, ATT_TK), :], (((1,), (1,)), ((), ())),
                               preferred_element_type=F32)

    def update(s, j):
        k0 = pl.multiple_of(j * ATT_TK, ATT_TK)
        m_prev = m_ref[...]
        m_new = jnp.maximum(m_prev, jnp.max(s, axis=-1, keepdims=True))
        alpha = jnp.exp2(m_prev - m_new)
        p = jnp.exp2(s - jnp.concatenate([m_new] * (ATT_TK // V7X_LANES), axis=1))
        pv = jnp.dot(p.astype(BF16), v_ref[pl.ds(k0, ATT_TK), :], preferred_element_type=F32)
        acc_ref[...] = jnp.concatenate([alpha] * (FOX_DP // V7X_LANES), axis=1) * acc_ref[...] + pv
        m_ref[...] = m_new

    def pair(jj, carry):
        s_a = logits(2 * jj)
        s_b = logits(2 * jj + 1)
        update(s_a, 2 * jj)
        update(s_b, 2 * jj + 1)
        return carry

    lax.fori_loop(0, i, pair, 0)

    row = lax.broadcasted_iota(jnp.int32, (ATT_TQ, ATT_TK), 0)
    col = lax.broadcasted_iota(jnp.int32, (ATT_TQ, ATT_TK), 1)
    update(jnp.where(row >= col, logits(2 * i), -jnp.inf), 2 * i)
    update(jnp.where(row >= col + ATT_TK, logits(2 * i + 1), -jnp.inf), 2 * i + 1)
    acc = acc_ref[...]
    o = acc[:, :FOX_DH] / acc[:, FOX_DH:]
    o_ref[...] = (o * jax.nn.sigmoid(og_ref[...])).astype(BF16)


def _fox_attn(qp, kp, vp, z):
    m = qp.shape[0]
    og_col0 = 3 * FOX_HEADS
    return pl.pallas_call(
        _fox_attn_kernel,
        grid=(FOX_HEADS, m // ATT_TQ),
        in_specs=[
            pl.BlockSpec((ATT_TQ, FOX_DP), lambda h, i: (i, h)),
            pl.BlockSpec((m, FOX_DP), lambda h, i: (0, h)),
            pl.BlockSpec((m, FOX_DP), lambda h, i: (0, h)),
            pl.BlockSpec((ATT_TQ, FOX_DH), lambda h, i: (i, og_col0 + h)),
        ],
        out_specs=pl.BlockSpec((ATT_TQ, FOX_DH), lambda h, i: (i, h)),
        out_shape=jax.ShapeDtypeStruct((m, FOX_D), BF16),
        scratch_shapes=[
            pltpu.VMEM((ATT_TQ, V7X_LANES), F32),
            pltpu.VMEM((ATT_TQ, FOX_DP), F32),
        ],
        compiler_params=_params(("parallel", "arbitrary")),
        name="fox_attn",
    )(qp, kp, vp, z)


def _pad_last(w, n):
    return jnp.pad(w, [(0, 0)] * (w.ndim - 1) + [(0, n - w.shape[-1])])


def _mixer_gla_conv(h, e, layer, p):
    z, zlr = _proj(h, p["mix_norm"], layer, p["ab_w_in"], e, AB_MAIN, p["ab_w_lr"])
    zc, = _proj(h, p["mix_norm"], layer, p["ab_w_conv"], e, 2 * CONV_DIM)
    mix = _gla_conv(z, zc, zlr, p["gla_w_gk2"][e], p["gla_b_gk"][e][None, :], p["gla_out_norm"][e][None, :],
                    p["conv_w"][e], p["conv_b"][e][None, :], p["conv_ln_g"][e][None, :],
                    p["conv_ln_b"][e][None, :])
    return _out_proj(mix, p["ab_w_out"], h, e)


def _mixer_fox(h, o, layer, p):
    z, zf = _proj(h, p["mix_norm"], layer, p["fox_w_in"], o, 4 * FOX_D, p["fox_w_f"])
    cum = _fox_cum(zf, p["fox_b_f"][o][None, :])
    qp, kp, vp = _fox_prep(z, cum, p["fox_q_norm"][o][None, :], p["fox_k_norm"][o][None, :])
    att = _fox_attn(qp, kp, vp, z)
    return _out_proj(att, p["fox_w_out"], h, o)


def kernel(x, ffn1_norm, ffn1_gate, ffn1_up, ffn1_down, mix_norm, ffn2_norm, ffn2_gate, ffn2_up,
           ffn2_down, ab_w_in, gla_w_gk2, gla_b_gk, gla_out_norm, conv_w, conv_b, conv_ln_g,
           conv_ln_b, ab_w_out, fox_w_in, fox_b_f, fox_q_norm, fox_k_norm, fox_w_out):
    b, t, d = x.shape
    assert b == 1, "the time recurrences assume one sequence"
    lr0 = AB_MAIN
    p = dict(
        mix_norm=mix_norm[:, None, :],
        ab_w_in=ab_w_in,
        ab_w_conv=ab_w_in[:, :, lr0 + GLA_RANK:].astype(BF16),
        ab_w_lr=_pad_last(ab_w_in[:, :, lr0:lr0 + GLA_RANK], V7X_LANES).astype(BF16),
        gla_w_gk2=jnp.pad(gla_w_gk2, ((0, 0), (0, V7X_LANES - GLA_RANK), (0, 0))).astype(BF16),
        gla_b_gk=gla_b_gk, gla_out_norm=gla_out_norm, conv_w=conv_w, conv_b=conv_b,
        conv_ln_g=conv_ln_g, conv_ln_b=conv_ln_b, ab_w_out=ab_w_out.astype(BF16),
        fox_w_in=fox_w_in,
        fox_w_f=_pad_last(fox_w_in[:, :, 4 * FOX_D:], V7X_LANES).astype(BF16),
        fox_b_f=_pad_last(fox_b_f, V7X_LANES), fox_q_norm=fox_q_norm, fox_k_norm=fox_k_norm,
        fox_w_out=fox_w_out.astype(BF16),
    )
    ffn1 = (ffn1_norm[:, None, :], ffn1_gate, ffn1_up, ffn1_down)
    ffn2 = (ffn2_norm[:, None, :], ffn2_gate, ffn2_up, ffn2_down)
    h = x.reshape(b * t, d)
    for layer in range(DEPTH):
        h = _ffn(h, *ffn1, layer)
        if layer % 2 == 0:
            h = _mixer_gla_conv(h, layer // 2, layer, p)
        else:
            h = _mixer_fox(h, layer // 2, layer, p)
        h = _ffn(h, *ffn2, layer)
    return h.reshape(b, t, d)
```

```python
import functools

import jax
import jax.numpy as jnp
from jax import lax
from jax.experimental import pallas as pl
from jax.experimental.pallas import tpu as pltpu

F32 = jnp.float32
BF16 = jnp.bfloat16

D_MODEL = 2048
SEQ = 8192
DEPTH = 4
D_FF = 5632
GLA_HEADS = 4
GLA_DK = 128
GLA_DV = 256
GLA_QK = GLA_HEADS * GLA_DK
GLA_V = GLA_HEADS * GLA_DV
GLA_RANK = 16
GLA_GATE_TAU = 16.0
GLA_CHUNK = 64
CONV_DIM = D_MODEL // 2
CONV_WIDTH = 31
AB_MAIN = 2 * GLA_QK + 2 * GLA_V
FOX_HEADS = 16
FOX_DH = 128
FOX_D = FOX_HEADS * FOX_DH
EPS = 1e-6

V7X_LANES = 128
V7X_SUBLANES = 8
V7X_VMEM_BYTES = 64 * 1024 * 1024
VMEM_LIMIT = V7X_VMEM_BYTES * 7 // 8

CONV_HALO = 32


def _params(semantics, vmem=VMEM_LIMIT):
    return pltpu.CompilerParams(dimension_semantics=semantics, vmem_limit_bytes=vmem)


def _rms_normed(x, gain):
    ms = jnp.mean(x * x, axis=-1, keepdims=True)
    return x * lax.rsqrt(ms + EPS) * gain


def _log_sigmoid(x):
    return jnp.minimum(x, 0.0) - jnp.log1p(jnp.exp(-jnp.abs(x)))


def _silu(x):
    return x * jax.nn.sigmoid(x)


def _lower_tri(n):
    r = lax.broadcasted_iota(jnp.int32, (n, n), 0)
    c = lax.broadcasted_iota(jnp.int32, (n, n), 1)
    return r >= c


FFN_TM = 1024
FFN_TF = 512
FFN_HEAD_TF = 256


def _ffn_body(h_ref, gain_ref, load_weights, o_ref, xn_ref):
    f = pl.program_id(1)

    @pl.when(f == 0)
    def _():
        xn_ref[...] = _rms_normed(h_ref[...], gain_ref[...]).astype(BF16)
        o_ref[...] = jnp.zeros_like(o_ref)

    wg, wu, wd = load_weights()
    x = xn_ref[...]
    g = jnp.dot(x, wg, preferred_element_type=F32)
    u = jnp.dot(x, wu, preferred_element_type=F32)
    a = (_silu(g) * u).astype(BF16)
    o_ref[...] += jnp.dot(a, wd, preferred_element_type=F32)

    @pl.when(f == pl.num_programs(1) - 1)
    def _():
        o_ref[...] = h_ref[...] + 0.5 * o_ref[...]


def _ffn_head_kernel(h_ref, gain_ref, wg_ref, wu_ref, wd_ref, o_ref, wg_out, wu_out, wd_out, xn_ref):
    def load_weights():
        wg, wu, wd = wg_ref[...].astype(BF16), wu_ref[...].astype(BF16), wd_ref[...].astype(BF16)
        wg_out[...] = wg
        wu_out[...] = wu
        wd_out[...] = wd
        return wg, wu, wd

    _ffn_body(h_ref, gain_ref, load_weights, o_ref, xn_ref)


def _ffn_tail_kernel(h_ref, gain_ref, wg_ref, wu_ref, wd_ref, head_ref, o_ref, xn_ref, sem):
    i = pl.program_id(0)

    @pl.when(jnp.logical_and(i == 0, pl.program_id(1) == 0))
    def _():
        copy = pltpu.make_async_copy(head_ref, o_ref, sem)
        copy.start()
        copy.wait()

    @pl.when(i > 0)
    def _():
        _ffn_body(h_ref, gain_ref, lambda: (wg_ref[...], wu_ref[...], wd_ref[...]), o_ref, xn_ref)


def _ffn(h, gain, wg, wu, wd, layer):
    m, d = h.shape
    dff = wg.shape[2]
    tfh = FFN_HEAD_TF
    once = pl.Buffered(1)
    head, wg16, wu16, wd16 = pl.pallas_call(
        _ffn_head_kernel,
        grid=(1, dff // tfh),
        in_specs=[
            pl.BlockSpec((FFN_TM, d), lambda i, f: (0, 0), pipeline_mode=once),
            pl.BlockSpec((None, 1, d), lambda i, f: (layer, 0, 0)),
            pl.BlockSpec((None, d, tfh), lambda i, f: (layer, 0, f)),
            pl.BlockSpec((None, d, tfh), lambda i, f: (layer, 0, f)),
            pl.BlockSpec((None, tfh, d), lambda i, f: (layer, f, 0)),
        ],
        out_specs=[
            pl.BlockSpec((FFN_TM, d), lambda i, f: (0, 0), pipeline_mode=once),
            pl.BlockSpec((d, tfh), lambda i, f: (0, f)),
            pl.BlockSpec((d, tfh), lambda i, f: (0, f)),
            pl.BlockSpec((tfh, d), lambda i, f: (f, 0)),
        ],
        out_shape=[
            jax.ShapeDtypeStruct((FFN_TM, d), F32),
            jax.ShapeDtypeStruct((d, dff), BF16),
            jax.ShapeDtypeStruct((d, dff), BF16),
            jax.ShapeDtypeStruct((dff, d), BF16),
        ],
        scratch_shapes=[pltpu.VMEM((FFN_TM, d), BF16)],
        compiler_params=_params(("arbitrary", "arbitrary")),
        name="ffn_head",
    )(h, gain, wg, wu, wd)
    wcol = lambda i, f: (0, jnp.where(i == 0, 0, f))
    return pl.pallas_call(
        _ffn_tail_kernel,
        grid=(m // FFN_TM, dff // FFN_TF),
        in_specs=[
            pl.BlockSpec((FFN_TM, d), lambda i, f: (jnp.maximum(i, 1), 0)),
            pl.BlockSpec((None, 1, d), lambda i, f: (layer, 0, 0)),
            pl.BlockSpec((d, FFN_TF), wcol),
            pl.BlockSpec((d, FFN_TF), wcol),
            pl.BlockSpec((FFN_TF, d), lambda i, f: (jnp.where(i == 0, 0, f), 0)),
            pl.BlockSpec(memory_space=pl.ANY),
        ],
        out_specs=pl.BlockSpec((FFN_TM, d), lambda i, f: (i, 0)),
        out_shape=jax.ShapeDtypeStruct((m, d), F32),
        scratch_shapes=[pltpu.VMEM((FFN_TM, d), BF16), pltpu.SemaphoreType.DMA(())],
        compiler_params=_params(("arbitrary", "arbitrary")),
        name="ffn",
    )(h, gain, wg16, wu16, wd16, head)


PROJ_TM = 1024
PROJ_TN = 1024


def _proj_kernel(has_side, h_ref, gain_ref, w_ref, *rest):
    if has_side:
        ws_ref, o_ref, os_ref, xn_ref = rest
    else:
        o_ref, xn_ref = rest

    @pl.when(pl.program_id(1) == 0)
    def _():
        xn_ref[...] = _rms_normed(h_ref[...], gain_ref[...]).astype(BF16)
        if has_side:
            os_ref[...] = jnp.dot(xn_ref[...], ws_ref[...], preferred_element_type=F32)

    o_ref[...] = jnp.dot(xn_ref[...], w_ref[...], preferred_element_type=F32)


def _proj(h, gain, layer, w, idx, n, w_side=None):
    m, d = h.shape
    in_specs = [
        pl.BlockSpec((PROJ_TM, d), lambda i, j: (i, 0)),
        pl.BlockSpec((None, 1, d), lambda i, j: (layer, 0, 0)),
        pl.BlockSpec((None, d, PROJ_TN), lambda i, j: (idx, 0, j)),
    ]
    out_specs = [pl.BlockSpec((PROJ_TM, PROJ_TN), lambda i, j: (i, j))]
    out_shape = [jax.ShapeDtypeStruct((m, n), F32)]
    args = [h, gain, w]
    if w_side is not None:
        ns = w_side.shape[2]
        in_specs.append(pl.BlockSpec((None, d, ns), lambda i, j: (idx, 0, 0)))
        out_specs.append(pl.BlockSpec((PROJ_TM, ns), lambda i, j: (i, 0)))
        out_shape.append(jax.ShapeDtypeStruct((m, ns), F32))
        args.append(w_side)
    return pl.pallas_call(
        functools.partial(_proj_kernel, w_side is not None),
        grid=(m // PROJ_TM, n // PROJ_TN),
        in_specs=in_specs,
        out_specs=out_specs,
        out_shape=out_shape,
        scratch_shapes=[pltpu.VMEM((PROJ_TM, d), BF16)],
        compiler_params=_params(("parallel", "arbitrary")),
        name="proj",
    )(*args)


OUT_TM = 512
OUT_TN = 2048


def _out_proj_kernel(a_ref, w_ref, h_ref, o_ref):
    o_ref[...] = h_ref[...] + jnp.dot(a_ref[...], w_ref[...], preferred_element_type=F32)


def _out_proj(a, w, h, idx):
    m, k = a.shape
    n = w.shape[2]
    return pl.pallas_call(
        _out_proj_kernel,
        grid=(m // OUT_TM, n // OUT_TN),
        in_specs=[
            pl.BlockSpec((OUT_TM, k), lambda i, j: (i, 0)),
            pl.BlockSpec((None, k, OUT_TN), lambda i, j: (idx, 0, j)),
            pl.BlockSpec((OUT_TM, OUT_TN), lambda i, j: (i, j)),
        ],
        out_specs=pl.BlockSpec((OUT_TM, OUT_TN), lambda i, j: (i, j)),
        out_shape=jax.ShapeDtypeStruct((m, n), F32),
        compiler_params=_params(("parallel", "parallel")),
        name="out_proj",
    )(a, w, h)


MIX_TT = 512
CONV_RB = 128
CONV_LW = 128


def _gla_conv_kernel(q_ref, k_ref, v_ref, g_ref, cv_ref, cg_ref, lr_ref,
                     wgk_ref, bgk_ref, gnorm_ref, cw_ref, cb_ref, lng_ref, lnb_ref,
                     mix_ref, st_ref, cbuf_ref, la_ref):
    @pl.when(pl.program_id(0) == 0)
    def _():
        st_ref[...] = jnp.zeros_like(st_ref)
        cbuf_ref[0:CONV_HALO, :] = jnp.zeros((CONV_HALO, CONV_DIM), F32)

    gk = jnp.dot(lr_ref[...].astype(BF16), wgk_ref[...], preferred_element_type=F32) + bgk_ref[...]
    la_ref[...] = _log_sigmoid(gk) * (1.0 / GLA_GATE_TAU)

    tri = _lower_tri(GLA_CHUNK)
    tri_f = tri.astype(F32)
    gnorm = gnorm_ref[...]
    mid = GLA_CHUNK // 2

    def chunk(c, carry):
        r0 = pl.multiple_of(c * GLA_CHUNK, GLA_CHUNK)
        rows = pl.ds(r0, GLA_CHUNK)
        cum = jnp.dot(tri_f, la_ref[rows, :], preferred_element_type=F32,
                      precision=lax.Precision.HIGHEST)
        for h in range(GLA_HEADS):
            ks = slice(h * GLA_DK, (h + 1) * GLA_DK)
            vs = slice(h * GLA_DV, (h + 1) * GLA_DV)
            cum_h = cum[:, ks]
            ref = cum_h[mid:mid + 1, :]
            last = cum_h[GLA_CHUNK - 1:GLA_CHUNK, :]
            q = q_ref[rows, ks] * (GLA_DK ** -0.5)
            k = k_ref[rows, ks]
            v = v_ref[rows, vs].astype(BF16)
            qt = (q * jnp.exp(cum_h - ref)).astype(BF16)
            kt = (k * jnp.exp(ref - cum_h)).astype(BF16)
            a = lax.dot_general(qt, kt, (((1,), (1,)), ((), ())), preferred_element_type=F32)
            a = jnp.where(tri, a, 0.0).astype(BF16)
            o = jnp.dot(a, v, preferred_element_type=F32)
            st = st_ref[h]
            q_in = (q * jnp.exp(cum_h)).astype(BF16)
            o = o + lax.dot_general(q_in, st.astype(BF16), (((1,), (1,)), ((), ())),
                                    preferred_element_type=F32)
            k_out = (k * jnp.exp(last - cum_h)).astype(BF16)
            st_ref[h] = st * jnp.exp(last) + lax.dot_general(
                v, k_out, (((0,), (0,)), ((), ())), preferred_element_type=F32)
            on = _rms_normed(o, gnorm)
            mix_ref[rows, vs] = (on * _silu(g_ref[rows, vs])).astype(BF16)
        return carry

    lax.fori_loop(0, MIX_TT // GLA_CHUNK, chunk, 0, unroll=4)

    cbuf_ref[CONV_HALO:CONV_HALO + MIX_TT, :] = cv_ref[...] * jax.nn.sigmoid(cg_ref[...])
    cb = cb_ref[...]
    lng = lng_ref[...]
    lnb = lnb_ref[...]
    tap0 = CONV_HALO - (CONV_WIDTH - 1)

    def conv_rows(rb, carry):
        base = pl.multiple_of(rb * CONV_RB, CONV_RB)
        halves = []
        for lane0 in range(0, CONV_DIM, CONV_LW):
            ls = slice(lane0, lane0 + CONV_LW)
            acc = None
            for s in range(V7X_SUBLANES):
                part = None
                rows = CONV_RB + (V7X_SUBLANES if s else 0)
                for d in range(s, tap0 + CONV_WIDTH, V7X_SUBLANES):
                    if d < tap0:
                        continue
                    win = cbuf_ref[pl.ds(base + (d - s), rows), ls]
                    term = cw_ref[d - tap0:d - tap0 + 1, ls] * win
                    part = term if part is None else part + term
                part = part[s:s + CONV_RB, :]
                acc = part if acc is None else acc + part
            halves.append(acc)
        y = jnp.concatenate(halves, axis=1) + cb
        mu = jnp.mean(y, axis=-1, keepdims=True)
        yc = y - mu
        var = jnp.mean(yc * yc, axis=-1, keepdims=True)
        yn = yc * lax.rsqrt(var + EPS) * lng + lnb
        mix_ref[pl.ds(base, CONV_RB), GLA_V:GLA_V + CONV_DIM] = _silu(yn).astype(BF16)
        return carry

    lax.fori_loop(0, MIX_TT // CONV_RB, conv_rows, 0)
    cbuf_ref[0:CONV_HALO, :] = cbuf_ref[MIX_TT:MIX_TT + CONV_HALO, :]


def _gla_conv(z, zc, zlr, wgk, bgk, gnorm, cw, cb, lng, lnb):
    m = z.shape[0]
    row = lambda shape: pl.BlockSpec(shape, lambda t: (0, 0))
    return pl.pallas_call(
        _gla_conv_kernel,
        grid=(m // MIX_TT,),
        in_specs=[
            pl.BlockSpec((MIX_TT, GLA_QK), lambda t: (t, 0)),
            pl.BlockSpec((MIX_TT, GLA_QK), lambda t: (t, 1)),
            pl.BlockSpec((MIX_TT, GLA_V), lambda t: (t, 1)),
            pl.BlockSpec((MIX_TT, GLA_V), lambda t: (t, 2)),
            pl.BlockSpec((MIX_TT, CONV_DIM), lambda t: (t, 0)),
            pl.BlockSpec((MIX_TT, CONV_DIM), lambda t: (t, 1)),
            pl.BlockSpec((MIX_TT, V7X_LANES), lambda t: (t, 0)),
            row((V7X_LANES, GLA_QK)),
            row((1, GLA_QK)),
            row((1, GLA_DV)),
            row((CONV_WIDTH, CONV_DIM)),
            row((1, CONV_DIM)),
            row((1, CONV_DIM)),
            row((1, CONV_DIM)),
        ],
        out_specs=pl.BlockSpec((MIX_TT, GLA_V + CONV_DIM), lambda t: (t, 0)),
        out_shape=jax.ShapeDtypeStruct((m, GLA_V + CONV_DIM), BF16),
        scratch_shapes=[
            pltpu.VMEM((GLA_HEADS, GLA_DV, GLA_DK), F32),
            pltpu.VMEM((CONV_HALO + MIX_TT, CONV_DIM), F32),
            pltpu.VMEM((MIX_TT, GLA_QK), F32),
        ],
        compiler_params=_params(("arbitrary",)),
        name="gla_conv",
    )(z, z, z, z, zc, zc, zlr, wgk, bgk, gnorm, cw, cb, lng, lnb)


PREP_TM = 256
CUM_TB = 256
ATT_TQ = 1024
ATT_TK = 512
FOX_DP = 2 * FOX_DH
LOG2E = 1.4426950408889634


def _split3(c):
    hi = c.astype(BF16).astype(F32)
    r = c - hi
    mid = r.astype(BF16).astype(F32)
    lo = (r - mid).astype(BF16).astype(F32)
    return hi, mid, lo


def _fox_prep_kernel(zq_ref, zk_ref, zv_ref, cum_ref, qg_ref, kg_ref, qp_ref, kp_ref, vp_ref):
    qg = qg_ref[...] * (FOX_DH ** -0.5 * LOG2E)
    kg = kg_ref[...]
    lane = lax.broadcasted_iota(jnp.int32, (PREP_TM, V7X_LANES), 1)
    ones = jnp.ones((PREP_TM, V7X_LANES), BF16)
    cum = cum_ref[...] * LOG2E
    for h in range(FOX_HEADS):
        hs = slice(h * FOX_DH, (h + 1) * FOX_DH)
        f0 = h * FOX_DP
        hi, mid, lo = _split3(jnp.broadcast_to(cum[:, h:h + 1], (PREP_TM, V7X_LANES)))
        qb = jnp.where(lane == 0, hi, jnp.where(lane == 1, mid, jnp.where(lane == 2, lo,
             jnp.where(lane < 6, 1.0, 0.0))))
        kb = jnp.where(lane < 3, 1.0, jnp.where(lane == 3, -hi, jnp.where(lane == 4, -mid,
             jnp.where(lane == 5, -lo, 0.0))))
        qp_ref[:, f0:f0 + FOX_DH] = _rms_normed(zq_ref[:, hs], qg).astype(BF16)
        qp_ref[:, f0 + FOX_DH:f0 + FOX_DP] = qb.astype(BF16)
        kp_ref[:, f0:f0 + FOX_DH] = _rms_normed(zk_ref[:, hs], kg).astype(BF16)
        kp_ref[:, f0 + FOX_DH:f0 + FOX_DP] = kb.astype(BF16)
        vp_ref[:, f0:f0 + FOX_DH] = zv_ref[:, hs].astype(BF16)
        vp_ref[:, f0 + FOX_DH:f0 + FOX_DP] = ones


def _fox_prep(z, cum, qg, kg):
    m = z.shape[0]
    blk = lambda c: pl.BlockSpec((PREP_TM, FOX_D), lambda i, c=c: (i, c))
    vec = pl.BlockSpec((1, FOX_DH), lambda i: (0, 0))
    out_spec = pl.BlockSpec((PREP_TM, FOX_HEADS * FOX_DP), lambda i: (i, 0))
    out = jax.ShapeDtypeStruct((m, FOX_HEADS * FOX_DP), BF16)
    return pl.pallas_call(
        _fox_prep_kernel,
        grid=(m // PREP_TM,),
        in_specs=[blk(0), blk(1), blk(2), pl.BlockSpec((PREP_TM, V7X_LANES), lambda i: (i, 0)), vec, vec],
        out_specs=[out_spec, out_spec, out_spec],
        out_shape=[out, out, out],
        compiler_params=_params(("parallel",)),
        name="fox_prep",
    )(z, z, z, cum, qg, kg)


def _fox_cum_kernel(f_ref, b_ref, o_ref, carry_ref):
    @pl.when(pl.program_id(0) == 0)
    def _():
        carry_ref[...] = jnp.zeros_like(carry_ref)

    lf = _log_sigmoid(f_ref[...] + b_ref[...])
    cum = jnp.dot(_lower_tri(CUM_TB).astype(F32), lf, preferred_element_type=F32,
                  precision=lax.Precision.HIGHEST) + carry_ref[0:1, :]
    o_ref[...] = cum
    carry_ref[...] = jnp.broadcast_to(cum[CUM_TB - 1:CUM_TB, :], carry_ref.shape)


def _fox_cum(zf, bf):
    m, n = zf.shape
    return pl.pallas_call(
        _fox_cum_kernel,
        grid=(m // CUM_TB,),
        in_specs=[pl.BlockSpec((CUM_TB, n), lambda i: (i, 0)), pl.BlockSpec((1, n), lambda i: (0, 0))],
        out_specs=pl.BlockSpec((CUM_TB, n), lambda i: (i, 0)),
        out_shape=jax.ShapeDtypeStruct((m, n), F32),
        scratch_shapes=[pltpu.VMEM((V7X_SUBLANES, n), F32)],
        compiler_params=_params(("arbitrary",)),
        name="fox_cum",
    )(zf, bf)


def _fox_attn_kernel(q_ref, k_ref, v_ref, og_ref, o_ref, m_ref, acc_ref):
    i = pl.program_id(1)
    q = q_ref[...]
    m_ref[...] = jnp.full_like(m_ref, -jnp.inf)
    acc_ref[...] = jnp.zeros_like(acc_ref)

    def logits(j):
        k0 = pl.multiple_of(j * ATT_TK, ATT_TK)
        return lax.dot_general(q, k_ref[pl.ds(k0, ATT_TK), :], (((1,), (1,)), ((), ())),
                               preferred_element_type=F32)

    def update(s, j):
        k0 = pl.multiple_of(j * ATT_TK, ATT_TK)
        m_prev = m_ref[...]
        m_new = jnp.maximum(m_prev, jnp.max(s, axis=-1, keepdims=True))
        alpha = jnp.exp2(m_prev - m_new)
        p = jnp.exp2(s - jnp.concatenate([m_new] * (ATT_TK // V7X_LANES), axis=1))
        pv = jnp.dot(p.astype(BF16), v_ref[pl.ds(k0, ATT_TK), :], preferred_element_type=F32)
        acc_ref[...] = jnp.concatenate([alpha] * (FOX_DP // V7X_LANES), axis=1) * acc_ref[...] + pv
        m_ref[...] = m_new

    def pair(jj, carry):
        s_a = logits(2 * jj)
        s_b = logits(2 * jj + 1)
        update(s_a, 2 * jj)
        update(s_b, 2 * jj + 1)
        return carry

    lax.fori_loop(0, i, pair, 0)

    row = lax.broadcasted_iota(jnp.int32, (ATT_TQ, ATT_TK), 0)
    col = lax.broadcasted_iota(jnp.int32, (ATT_TQ, ATT_TK), 1)
    update(jnp.where(row >= col, logits(2 * i), -jnp.inf), 2 * i)
    update(jnp.where(row >= col + ATT_TK, logits(2 * i + 1), -jnp.inf), 2 * i + 1)
    acc = acc_ref[...]
    o = acc[:, :FOX_DH] / acc[:, FOX_DH:]
    o_ref[...] = (o * jax.nn.sigmoid(og_ref[...])).astype(BF16)


def _fox_attn(qp, kp, vp, z):
    m = qp.shape[0]
    og_col0 = 3 * FOX_HEADS
    return pl.pallas_call(
        _fox_attn_kernel,
        grid=(FOX_HEADS, m // ATT_TQ),
        in_specs=[
            pl.BlockSpec((ATT_TQ, FOX_DP), lambda h, i: (i, h)),
            pl.BlockSpec((m, FOX_DP), lambda h, i: (0, h)),
            pl.BlockSpec((m, FOX_DP), lambda h, i: (0, h)),
            pl.BlockSpec((ATT_TQ, FOX_DH), lambda h, i: (i, og_col0 + h)),
        ],
        out_specs=pl.BlockSpec((ATT_TQ, FOX_DH), lambda h, i: (i, h)),
        out_shape=jax.ShapeDtypeStruct((m, FOX_D), BF16),
        scratch_shapes=[
            pltpu.VMEM((ATT_TQ, V7X_LANES), F32),
            pltpu.VMEM((ATT_TQ, FOX_DP), F32),
        ],
        compiler_params=_params(("parallel", "arbitrary")),
        name="fox_attn",
    )(qp, kp, vp, z)


def _pad_last(w, n):
    return jnp.pad(w, [(0, 0)] * (w.ndim - 1) + [(0, n - w.shape[-1])])


def _mixer_gla_conv(h, e, layer, p):
    z, zlr = _proj(h, p["mix_norm"], layer, p["ab_w_in"], e, AB_MAIN, p["ab_w_lr"])
    zc, = _proj(h, p["mix_norm"], layer, p["ab_w_conv"], e, 2 * CONV_DIM)
    mix = _gla_conv(z, zc, zlr, p["gla_w_gk2"][e], p["gla_b_gk"][e][None, :], p["gla_out_norm"][e][None, :],
                    p["conv_w"][e], p["conv_b"][e][None, :], p["conv_ln_g"][e][None, :],
                    p["conv_ln_b"][e][None, :])
    return _out_proj(mix, p["ab_w_out"], h, e)


def _mixer_fox(h, o, layer, p):
    z, zf = _proj(h, p["mix_norm"], layer, p["fox_w_in"], o, 4 * FOX_D, p["fox_w_f"])
    cum = _fox_cum(zf, p["fox_b_f"][o][None, :])
    qp, kp, vp = _fox_prep(z, cum, p["fox_q_norm"][o][None, :], p["fox_k_norm"][o][None, :])
    att = _fox_attn(qp, kp, vp, z)
    return _out_proj(att, p["fox_w_out"], h, o)


def kernel(x, ffn1_norm, ffn1_gate, ffn1_up, ffn1_down, mix_norm, ffn2_norm, ffn2_gate, ffn2_up,
           ffn2_down, ab_w_in, gla_w_gk2, gla_b_gk, gla_out_norm, conv_w, conv_b, conv_ln_g,
           conv_ln_b, ab_w_out, fox_w_in, fox_b_f, fox_q_norm, fox_k_norm, fox_w_out):
    b, t, d = x.shape
    assert b == 1, "the time recurrences assume one sequence"
    lr0 = AB_MAIN
    p = dict(
        mix_norm=mix_norm[:, None, :],
        ab_w_in=ab_w_in.astype(BF16),
        ab_w_conv=ab_w_in[:, :, lr0 + GLA_RANK:].astype(BF16),
        ab_w_lr=_pad_last(ab_w_in[:, :, lr0:lr0 + GLA_RANK], V7X_LANES).astype(BF16),
        gla_w_gk2=jnp.pad(gla_w_gk2, ((0, 0), (0, V7X_LANES - GLA_RANK), (0, 0))).astype(BF16),
        gla_b_gk=gla_b_gk, gla_out_norm=gla_out_norm, conv_w=conv_w, conv_b=conv_b,
        conv_ln_g=conv_ln_g, conv_ln_b=conv_ln_b, ab_w_out=ab_w_out.astype(BF16),
        fox_w_in=fox_w_in.astype(BF16),
        fox_w_f=_pad_last(fox_w_in[:, :, 4 * FOX_D:], V7X_LANES).astype(BF16),
        fox_b_f=_pad_last(fox_b_f, V7X_LANES), fox_q_norm=fox_q_norm, fox_k_norm=fox_k_norm,
        fox_w_out=fox_w_out.astype(BF16),
    )
    ffn1 = (ffn1_norm[:, None, :], ffn1_gate, ffn1_up, ffn1_down)
    ffn2 = (ffn2_norm[:, None, :], ffn2_gate, ffn2_up, ffn2_down)
    h = x.reshape(b * t, d)
    for layer in range(DEPTH):
        h = _ffn(h, *ffn1, layer)
        if layer % 2 == 0:
            h = _mixer_gla_conv(h, layer // 2, layer, p)
        else:
            h = _mixer_fox(h, layer // 2, layer, p)
        h = _ffn(h, *ffn2, layer)
    return h.reshape(b, t, d)
```

```python
import functools

import jax
import jax.numpy as jnp
from jax import lax
from jax.experimental import pallas as pl
from jax.experimental.pallas import tpu as pltpu

F32 = jnp.float32
BF16 = jnp.bfloat16

D_MODEL = 2048
SEQ = 8192
DEPTH = 4
D_FF = 5632
GLA_HEADS = 4
GLA_DK = 128
GLA_DV = 256
GLA_QK = GLA_HEADS * GLA_DK
GLA_V = GLA_HEADS * GLA_DV
GLA_RANK = 16
GLA_GATE_TAU = 16.0
GLA_CHUNK = 64
CONV_DIM = D_MODEL // 2
CONV_WIDTH = 31
AB_MAIN = 2 * GLA_QK + 2 * GLA_V
FOX_HEADS = 16
FOX_DH = 128
FOX_D = FOX_HEADS * FOX_DH
EPS = 1e-6

V7X_LANES = 128
V7X_SUBLANES = 8
V7X_VMEM_BYTES = 64 * 1024 * 1024
VMEM_LIMIT = V7X_VMEM_BYTES * 7 // 8

CONV_HALO = 32


def _params(semantics, vmem=VMEM_LIMIT):
    return pltpu.CompilerParams(dimension_semantics=semantics, vmem_limit_bytes=vmem)


def _rms_normed(x, gain):
    ms = jnp.mean(x * x, axis=-1, keepdims=True)
    return x * lax.rsqrt(ms + EPS) * gain


def _log_sigmoid(x):
    return jnp.minimum(x, 0.0) - jnp.log1p(jnp.exp(-jnp.abs(x)))


def _silu(x):
    return x * jax.nn.sigmoid(x)


def _lower_tri(n):
    r = lax.broadcasted_iota(jnp.int32, (n, n), 0)
    c = lax.broadcasted_iota(jnp.int32, (n, n), 1)
    return r >= c


FFN_TM = 1024
FFN_TF = 512
FFN_HEAD_TF = 256


def _ffn_body(h_ref, gain_ref, load_weights, o_ref, xn_ref):
    f = pl.program_id(1)

    @pl.when(f == 0)
    def _():
        xn_ref[...] = _rms_normed(h_ref[...], gain_ref[...]).astype(BF16)
        o_ref[...] = jnp.zeros_like(o_ref)

    wg, wu, wd = load_weights()
    x = xn_ref[...]
    g = jnp.dot(x, wg, preferred_element_type=F32)
    u = jnp.dot(x, wu, preferred_element_type=F32)
    a = (_silu(g) * u).astype(BF16)
    o_ref[...] += jnp.dot(a, wd, preferred_element_type=F32)

    @pl.when(f == pl.num_programs(1) - 1)
    def _():
        o_ref[...] = h_ref[...] + 0.5 * o_ref[...]


def _ffn_head_kernel(h_ref, gain_ref, wg_ref, wu_ref, wd_ref, o_ref, wg_out, wu_out, wd_out, xn_ref):
    def load_weights():
        wg, wu, wd = wg_ref[...].astype(BF16), wu_ref[...].astype(BF16), wd_ref[...].astype(BF16)
        wg_out[...] = wg
        wu_out[...] = wu
        wd_out[...] = wd
        return wg, wu, wd

    _ffn_body(h_ref, gain_ref, load_weights, o_ref, xn_ref)


def _ffn_tail_kernel(h_ref, gain_ref, wg_ref, wu_ref, wd_ref, head_ref, o_ref, xn_ref, sem):
    i = pl.program_id(0)

    @pl.when(jnp.logical_and(i == 0, pl.program_id(1) == 0))
    def _():
        copy = pltpu.make_async_copy(head_ref, o_ref, sem)
        copy.start()
        copy.wait()

    @pl.when(i > 0)
    def _():
        _ffn_body(h_ref, gain_ref, lambda: (wg_ref[...], wu_ref[...], wd_ref[...]), o_ref, xn_ref)


def _ffn(h, gain, wg, wu, wd, layer):
    m, d = h.shape
    dff = wg.shape[2]
    tfh = FFN_HEAD_TF
    once = pl.Buffered(1)
    head, wg16, wu16, wd16 = pl.pallas_call(
        _ffn_head_kernel,
        grid=(1, dff // tfh),
        in_specs=[
            pl.BlockSpec((FFN_TM, d), lambda i, f: (0, 0), pipeline_mode=once),
            pl.BlockSpec((None, 1, d), lambda i, f: (layer, 0, 0)),
            pl.BlockSpec((None, d, tfh), lambda i, f: (layer, 0, f)),
            pl.BlockSpec((None, d, tfh), lambda i, f: (layer, 0, f)),
            pl.BlockSpec((None, tfh, d), lambda i, f: (layer, f, 0)),
        ],
        out_specs=[
            pl.BlockSpec((FFN_TM, d), lambda i, f: (0, 0), pipeline_mode=once),
            pl.BlockSpec((d, tfh), lambda i, f: (0, f)),
            pl.BlockSpec((d, tfh), lambda i, f: (0, f)),
            pl.BlockSpec((tfh, d), lambda i, f: (f, 0)),
        ],
        out_shape=[
            jax.ShapeDtypeStruct((FFN_TM, d), F32),
            jax.ShapeDtypeStruct((d, dff), BF16),
            jax.ShapeDtypeStruct((d, dff), BF16),
            jax.ShapeDtypeStruct((dff, d), BF16),
        ],
        scratch_shapes=[pltpu.VMEM((FFN_TM, d), BF16)],
        compiler_params=_params(("arbitrary", "arbitrary")),
        name="ffn_head",
    )(h, gain, wg, wu, wd)
    wcol = lambda i, f: (0, jnp.where(i == 0, 0, f))
    return pl.pallas_call(
        _ffn_tail_kernel,
        grid=(m // FFN_TM, dff // FFN_TF),
        in_specs=[
            pl.BlockSpec((FFN_TM, d), lambda i, f: (jnp.maximum(i, 1), 0)),
            pl.BlockSpec((None, 1, d), lambda i, f: (layer, 0, 0)),
            pl.BlockSpec((d, FFN_TF), wcol),
            pl.BlockSpec((d, FFN_TF), wcol),
            pl.BlockSpec((FFN_TF, d), lambda i, f: (jnp.where(i == 0, 0, f), 0)),
            pl.BlockSpec(memory_space=pl.ANY),
        ],
        out_specs=pl.BlockSpec((FFN_TM, d), lambda i, f: (i, 0)),
        out_shape=jax.ShapeDtypeStruct((m, d), F32),
        scratch_shapes=[pltpu.VMEM((FFN_TM, d), BF16), pltpu.SemaphoreType.DMA(())],
        compiler_params=_params(("arbitrary", "arbitrary")),
        name="ffn",
    )(h, gain, wg16, wu16, wd16, head)


PROJ_TM = 1024
PROJ_TN = 1024


def _proj_kernel(has_side, h_ref, gain_ref, w_ref, *rest):
    if has_side:
        ws_ref, o_ref, os_ref, xn_ref = rest
    else:
        o_ref, xn_ref = rest

    @pl.when(pl.program_id(1) == 0)
    def _():
        xn_ref[...] = _rms_normed(h_ref[...], gain_ref[...]).astype(BF16)
        if has_side:
            os_ref[...] = jnp.dot(xn_ref[...], ws_ref[...], preferred_element_type=F32)

    o_ref[...] = jnp.dot(xn_ref[...], w_ref[...], preferred_element_type=F32)


def _proj(h, gain, layer, w, idx, n, w_side=None):
    m, d = h.shape
    in_specs = [
        pl.BlockSpec((PROJ_TM, d), lambda i, j: (i, 0)),
        pl.BlockSpec((None, 1, d), lambda i, j: (layer, 0, 0)),
        pl.BlockSpec((None, d, PROJ_TN), lambda i, j: (idx, 0, j)),
    ]
    out_specs = [pl.BlockSpec((PROJ_TM, PROJ_TN), lambda i, j: (i, j))]
    out_shape = [jax.ShapeDtypeStruct((m, n), F32)]
    args = [h, gain, w]
    if w_side is not None:
        ns = w_side.shape[2]
        in_specs.append(pl.BlockSpec((None, d, ns), lambda i, j: (idx, 0, 0)))
        out_specs.append(pl.BlockSpec((PROJ_TM, ns), lambda i, j: (i, 0)))
        out_shape.append(jax.ShapeDtypeStruct((m, ns), F32))
        args.append(w_side)
    return pl.pallas_call(
        functools.partial(_proj_kernel, w_side is not None),
        grid=(m // PROJ_TM, n // PROJ_TN),
        in_specs=in_specs,
        out_specs=out_specs,
        out_shape=out_shape,
        scratch_shapes=[pltpu.VMEM((PROJ_TM, d), BF16)],
        compiler_params=_params(("parallel", "arbitrary")),
        name="proj",
    )(*args)


OUT_TM = 512
OUT_TN = 2048


def _out_proj_kernel(a_ref, w_ref, h_ref, o_ref):
    o_ref[...] = h_ref[...] + jnp.dot(a_ref[...], w_ref[...], preferred_element_type=F32)


def _out_proj(a, w, h, idx):
    m, k = a.shape
    n = w.shape[2]
    return pl.pallas_call(
        _out_proj_kernel,
        grid=(m // OUT_TM, n // OUT_TN),
        in_specs=[
            pl.BlockSpec((OUT_TM, k), lambda i, j: (i, 0)),
            pl.BlockSpec((None, k, OUT_TN), lambda i, j: (idx, 0, j)),
            pl.BlockSpec((OUT_TM, OUT_TN), lambda i, j: (i, j)),
        ],
        out_specs=pl.BlockSpec((OUT_TM, OUT_TN), lambda i, j: (i, j)),
        out_shape=jax.ShapeDtypeStruct((m, n), F32),
        compiler_params=_params(("parallel", "parallel")),
        name="out_proj",
    )(a, w, h)


MIX_TT = 512
CONV_RB = 128
CONV_LW = 128


def _gla_conv_kernel(q_ref, k_ref, v_ref, g_ref, cv_ref, cg_ref, lr_ref,
                     wgk_ref, bgk_ref, gnorm_ref, cw_ref, cb_ref, lng_ref, lnb_ref,
                     mix_ref, st_ref, cbuf_ref, la_ref):
    @pl.when(pl.program_id(0) == 0)
    def _():
        st_ref[...] = jnp.zeros_like(st_ref)
        cbuf_ref[0:CONV_HALO, :] = jnp.zeros((CONV_HALO, CONV_DIM), F32)

    gk = jnp.dot(lr_ref[...].astype(BF16), wgk_ref[...], preferred_element_type=F32) + bgk_ref[...]
    la_ref[...] = _log_sigmoid(gk) * (1.0 / GLA_GATE_TAU)

    tri = _lower_tri(GLA_CHUNK)
    tri_f = tri.astype(F32)
    gnorm = gnorm_ref[...]
    mid = GLA_CHUNK // 2

    def chunk(c, carry):
        r0 = pl.multiple_of(c * GLA_CHUNK, GLA_CHUNK)
        rows = pl.ds(r0, GLA_CHUNK)
        cum = jnp.dot(tri_f, la_ref[rows, :], preferred_element_type=F32,
                      precision=lax.Precision.HIGHEST)
        for h in range(GLA_HEADS):
            ks = slice(h * GLA_DK, (h + 1) * GLA_DK)
            vs = slice(h * GLA_DV, (h + 1) * GLA_DV)
            cum_h = cum[:, ks]
            ref = cum_h[mid:mid + 1, :]
            last = cum_h[GLA_CHUNK - 1:GLA_CHUNK, :]
            q = q_ref[rows, ks] * (GLA_DK ** -0.5)
            k = k_ref[rows, ks]
            v = v_ref[rows, vs].astype(BF16)
            qt = (q * jnp.exp(cum_h - ref)).astype(BF16)
            kt = (k * jnp.exp(ref - cum_h)).astype(BF16)
            a = lax.dot_general(qt, kt, (((1,), (1,)), ((), ())), preferred_element_type=F32)
            a = jnp.where(tri, a, 0.0).astype(BF16)
            o = jnp.dot(a, v, preferred_element_type=F32)
            st = st_ref[h]
            q_in = (q * jnp.exp(cum_h)).astype(BF16)
            o = o + lax.dot_general(q_in, st.astype(BF16), (((1,), (1,)), ((), ())),
                                    preferred_element_type=F32)
            k_out = (k * jnp.exp(last - cum_h)).astype(BF16)
            st_ref[h] = st * jnp.exp(last) + lax.dot_general(
                v, k_out, (((0,), (0,)), ((), ())), preferred_element_type=F32)
            on = _rms_normed(o, gnorm)
            mix_ref[rows, vs] = (on * _silu(g_ref[rows, vs])).astype(BF16)
        return carry

    lax.fori_loop(0, MIX_TT // GLA_CHUNK, chunk, 0, unroll=True)

    cbuf_ref[CONV_HALO:CONV_HALO + MIX_TT, :] = cv_ref[...] * jax.nn.sigmoid(cg_ref[...])
    cb = cb_ref[...]
    lng = lng_ref[...]
    lnb = lnb_ref[...]
    tap0 = CONV_HALO - (CONV_WIDTH - 1)

    def conv_rows(rb, carry):
        base = pl.multiple_of(rb * CONV_RB, CONV_RB)
        halves = []
        for lane0 in range(0, CONV_DIM, CONV_LW):
            ls = slice(lane0, lane0 + CONV_LW)
            acc = None
            for s in range(V7X_SUBLANES):
                part = None
                rows = CONV_RB + (V7X_SUBLANES if s else 0)
                for d in range(s, tap0 + CONV_WIDTH, V7X_SUBLANES):
                    if d < tap0:
                        continue
                    win = cbuf_ref[pl.ds(base + (d - s), rows), ls]
                    term = cw_ref[d - tap0:d - tap0 + 1, ls] * win
                    part = term if part is None else part + term
                part = part[s:s + CONV_RB, :]
                acc = part if acc is None else acc + part
            halves.append(acc)
        y = jnp.concatenate(halves, axis=1) + cb
        mu = jnp.mean(y, axis=-1, keepdims=True)
        yc = y - mu
        var = jnp.mean(yc * yc, axis=-1, keepdims=True)
        yn = yc * lax.rsqrt(var + EPS) * lng + lnb
        mix_ref[pl.ds(base, CONV_RB), GLA_V:GLA_V + CONV_DIM] = _silu(yn).astype(BF16)
        return carry

    lax.fori_loop(0, MIX_TT // CONV_RB, conv_rows, 0)
    cbuf_ref[0:CONV_HALO, :] = cbuf_ref[MIX_TT:MIX_TT + CONV_HALO, :]


def _gla_conv(z, zc, zlr, wgk, bgk, gnorm, cw, cb, lng, lnb):
    m = z.shape[0]
    row = lambda shape: pl.BlockSpec(shape, lambda t: (0, 0))
    return pl.pallas_call(
        _gla_conv_kernel,
        grid=(m // MIX_TT,),
        in_specs=[
            pl.BlockSpec((MIX_TT, GLA_QK), lambda t: (t, 0)),
            pl.BlockSpec((MIX_TT, GLA_QK), lambda t: (t, 1)),
            pl.BlockSpec((MIX_TT, GLA_V), lambda t: (t, 1)),
            pl.BlockSpec((MIX_TT, GLA_V), lambda t: (t, 2)),
            pl.BlockSpec((MIX_TT, CONV_DIM), lambda t: (t, 0)),
            pl.BlockSpec((MIX_TT, CONV_DIM), lambda t: (t, 1)),
            pl.BlockSpec((MIX_TT, V7X_LANES), lambda t: (t, 0)),
            row((V7X_LANES, GLA_QK)),
            row((1, GLA_QK)),
            row((1, GLA_DV)),
            row((CONV_WIDTH, CONV_DIM)),
            row((1, CONV_DIM)),
            row((1, CONV_DIM)),
            row((1, CONV_DIM)),
        ],
        out_specs=pl.BlockSpec((MIX_TT, GLA_V + CONV_DIM), lambda t: (t, 0)),
        out_shape=jax.ShapeDtypeStruct((m, GLA_V + CONV_DIM), BF16),
        scratch_shapes=[
            pltpu.VMEM((GLA_HEADS, GLA_DV, GLA_DK), F32),
            pltpu.VMEM((CONV_HALO + MIX_TT, CONV_DIM), F32),
            pltpu.VMEM((MIX_TT, GLA_QK), F32),
        ],
        compiler_params=_params(("arbitrary",)),
        name="gla_conv",
    )(z, z, z, z, zc, zc, zlr, wgk, bgk, gnorm, cw, cb, lng, lnb)


PREP_TM = 256
CUM_TB = 256
ATT_TQ = 1024
ATT_TK = 512
FOX_DP = 2 * FOX_DH
LOG2E = 1.4426950408889634


def _split3(c):
    hi = c.astype(BF16).astype(F32)
    r = c - hi
    mid = r.astype(BF16).astype(F32)
    lo = (r - mid).astype(BF16).astype(F32)
    return hi, mid, lo


def _fox_prep_kernel(zq_ref, zk_ref, zv_ref, cum_ref, qg_ref, kg_ref, qp_ref, kp_ref, vp_ref):
    qg = qg_ref[...] * (FOX_DH ** -0.5 * LOG2E)
    kg = kg_ref[...]
    lane = lax.broadcasted_iota(jnp.int32, (PREP_TM, V7X_LANES), 1)
    ones = jnp.ones((PREP_TM, V7X_LANES), BF16)
    cum = cum_ref[...] * LOG2E
    for h in range(FOX_HEADS):
        hs = slice(h * FOX_DH, (h + 1) * FOX_DH)
        f0 = h * FOX_DP
        hi, mid, lo = _split3(jnp.broadcast_to(cum[:, h:h + 1], (PREP_TM, V7X_LANES)))
        qb = jnp.where(lane == 0, hi, jnp.where(lane == 1, mid, jnp.where(lane == 2, lo,
             jnp.where(lane < 6, 1.0, 0.0))))
        kb = jnp.where(lane < 3, 1.0, jnp.where(lane == 3, -hi, jnp.where(lane == 4, -mid,
             jnp.where(lane == 5, -lo, 0.0))))
        qp_ref[:, f0:f0 + FOX_DH] = _rms_normed(zq_ref[:, hs], qg).astype(BF16)
        qp_ref[:, f0 + FOX_DH:f0 + FOX_DP] = qb.astype(BF16)
        kp_ref[:, f0:f0 + FOX_DH] = _rms_normed(zk_ref[:, hs], kg).astype(BF16)
        kp_ref[:, f0 + FOX_DH:f0 + FOX_DP] = kb.astype(BF16)
        vp_ref[:, f0:f0 + FOX_DH] = zv_ref[:, hs].astype(BF16)
        vp_ref[:, f0 + FOX_DH:f0 + FOX_DP] = ones


def _fox_prep(z, cum, qg, kg):
    m = z.shape[0]
    blk = lambda c: pl.BlockSpec((PREP_TM, FOX_D), lambda i, c=c: (i, c))
    vec = pl.BlockSpec((1, FOX_DH), lambda i: (0, 0))
    out_spec = pl.BlockSpec((PREP_TM, FOX_HEADS * FOX_DP), lambda i: (i, 0))
    out = jax.ShapeDtypeStruct((m, FOX_HEADS * FOX_DP), BF16)
    return pl.pallas_call(
        _fox_prep_kernel,
        grid=(m // PREP_TM,),
        in_specs=[blk(0), blk(1), blk(2), pl.BlockSpec((PREP_TM, V7X_LANES), lambda i: (i, 0)), vec, vec],
        out_specs=[out_spec, out_spec, out_spec],
        out_shape=[out, out, out],
        compiler_params=_params(("parallel",)),
        name="fox_prep",
    )(z, z, z, cum, qg, kg)


def _fox_cum_kernel(f_ref, b_ref, o_ref, carry_ref):
    @pl.when(pl.program_id(0) == 0)
    def _():
        carry_ref[...] = jnp.zeros_like(carry_ref)

    lf = _log_sigmoid(f_ref[...] + b_ref[...])
    cum = jnp.dot(_lower_tri(CUM_TB).astype(F32), lf, preferred_element_type=F32,
                  precision=lax.Precision.HIGHEST) + carry_ref[0:1, :]
    o_ref[...] = cum
    carry_ref[...] = jnp.broadcast_to(cum[CUM_TB - 1:CUM_TB, :], carry_ref.shape)


def _fox_cum(zf, bf):
    m, n = zf.shape
    return pl.pallas_call(
        _fox_cum_kernel,
        grid=(m // CUM_TB,),
        in_specs=[pl.BlockSpec((CUM_TB, n), lambda i: (i, 0)), pl.BlockSpec((1, n), lambda i: (0, 0))],
        out_specs=pl.BlockSpec((CUM_TB, n), lambda i: (i, 0)),
        out_shape=jax.ShapeDtypeStruct((m, n), F32),
        scratch_shapes=[pltpu.VMEM((V7X_SUBLANES, n), F32)],
        compiler_params=_params(("arbitrary",)),
        name="fox_cum",
    )(zf, bf)


def _fox_attn_kernel(q_ref, k_ref, v_ref, og_ref, o_ref, m_ref, acc_ref):
    i = pl.program_id(1)
    q = q_ref[...]
    m_ref[...] = jnp.full_like(m_ref, -jnp.inf)
    acc_ref[...] = jnp.zeros_like(acc_ref)

    def logits(j, rows=slice(None)):
        k0 = pl.multiple_of(j * ATT_TK, ATT_TK)
        return lax.dot_general(q[rows], k_ref[pl.ds(k0, ATT_TK), :], (((1,), (1,)), ((), ())),
                               preferred_element_type=F32)

    def update(s, j, rows=slice(None)):
        k0 = pl.multiple_of(j * ATT_TK, ATT_TK)
        m_prev = m_ref[rows, :]
        m_new = jnp.maximum(m_prev, jnp.max(s, axis=-1, keepdims=True))
        alpha = jnp.exp2(m_prev - m_new)
        p = jnp.exp2(s - jnp.concatenate([m_new] * (ATT_TK // V7X_LANES), axis=1))
        pv = jnp.dot(p.astype(BF16), v_ref[pl.ds(k0, ATT_TK), :], preferred_element_type=F32)
        acc_ref[rows, :] = jnp.concatenate([alpha] * (FOX_DP // V7X_LANES), axis=1) * acc_ref[rows, :] + pv
        m_ref[rows, :] = m_new

    def pair(jj, carry):
        s_a = logits(2 * jj)
        s_b = logits(2 * jj + 1)
        update(s_a, 2 * jj)
        update(s_b, 2 * jj + 1)
        return carry

    lax.fori_loop(0, i, pair, 0)

    row = lax.broadcasted_iota(jnp.int32, (ATT_TQ, ATT_TK), 0)
    col = lax.broadcasted_iota(jnp.int32, (ATT_TQ, ATT_TK), 1)
    update(jnp.where(row >= col, logits(2 * i), -jnp.inf), 2 * i)
    late = slice(ATT_TK, ATT_TQ)
    update(jnp.where(_lower_tri(ATT_TK), logits(2 * i + 1, late), -jnp.inf), 2 * i + 1, late)
    acc = acc_ref[...]
    o = acc[:, :FOX_DH] / acc[:, FOX_DH:]
    o_ref[...] = (o * jax.nn.sigmoid(og_ref[...])).astype(BF16)


def _fox_attn(qp, kp, vp, z):
    m = qp.shape[0]
    og_col0 = 3 * FOX_HEADS
    return pl.pallas_call(
        _fox_attn_kernel,
        grid=(FOX_HEADS, m // ATT_TQ),
        in_specs=[
            pl.BlockSpec((ATT_TQ, FOX_DP), lambda h, i: (i, h)),
            pl.BlockSpec((m, FOX_DP), lambda h, i: (0, h)),
            pl.BlockSpec((m, FOX_DP), lambda h, i: (0, h)),
            pl.BlockSpec((ATT_TQ, FOX_DH), lambda h, i: (i, og_col0 + h)),
        ],
        out_specs=pl.BlockSpec((ATT_TQ, FOX_DH), lambda h, i: (i, h)),
        out_shape=jax.ShapeDtypeStruct((m, FOX_D), BF16),
        scratch_shapes=[
            pltpu.VMEM((ATT_TQ, V7X_LANES), F32),
            pltpu.VMEM((ATT_TQ, FOX_DP), F32),
        ],
        compiler_params=_params(("parallel", "arbitrary")),
        name="fox_attn",
    )(qp, kp, vp, z)


def _pad_last(w, n):
    return jnp.pad(w, [(0, 0)] * (w.ndim - 1) + [(0, n - w.shape[-1])])


def _mixer_gla_conv(h, e, layer, p):
    z, zlr = _proj(h, p["mix_norm"], layer, p["ab_w_in"], e, AB_MAIN, p["ab_w_lr"])
    zc, = _proj(h, p["mix_norm"], layer, p["ab_w_conv"], e, 2 * CONV_DIM)
    mix = _gla_conv(z, zc, zlr, p["gla_w_gk2"][e], p["gla_b_gk"][e][None, :], p["gla_out_norm"][e][None, :],
                    p["conv_w"][e], p["conv_b"][e][None, :], p["conv_ln_g"][e][None, :],
                    p["conv_ln_b"][e][None, :])
    return _out_proj(mix, p["ab_w_out"], h, e)


def _mixer_fox(h, o, layer, p):
    z, zf = _proj(h, p["mix_norm"], layer, p["fox_w_in"], o, 4 * FOX_D, p["fox_w_f"])
    cum = _fox_cum(zf, p["fox_b_f"][o][None, :])
    qp, kp, vp = _fox_prep(z, cum, p["fox_q_norm"][o][None, :], p["fox_k_norm"][o][None, :])
    att = _fox_attn(qp, kp, vp, z)
    return _out_proj(att, p["fox_w_out"], h, o)


def kernel(x, ffn1_norm, ffn1_gate, ffn1_up, ffn1_down, mix_norm, ffn2_norm, ffn2_gate, ffn2_up,
           ffn2_down, ab_w_in, gla_w_gk2, gla_b_gk, gla_out_norm, conv_w, conv_b, conv_ln_g,
           conv_ln_b, ab_w_out, fox_w_in, fox_b_f, fox_q_norm, fox_k_norm, fox_w_out):
    b, t, d = x.shape
    assert b == 1, "the time recurrences assume one sequence"
    lr0 = AB_MAIN
    p = dict(
        mix_norm=mix_norm[:, None, :],
        ab_w_in=ab_w_in.astype(BF16),
        ab_w_conv=ab_w_in[:, :, lr0 + GLA_RANK:].astype(BF16),
        ab_w_lr=_pad_last(ab_w_in[:, :, lr0:lr0 + GLA_RANK], V7X_LANES).astype(BF16),
        gla_w_gk2=jnp.pad(gla_w_gk2, ((0, 0), (0, V7X_LANES - GLA_RANK), (0, 0))).astype(BF16),
        gla_b_gk=gla_b_gk, gla_out_norm=gla_out_norm, conv_w=conv_w, conv_b=conv_b,
        conv_ln_g=conv_ln_g, conv_ln_b=conv_ln_b, ab_w_out=ab_w_out.astype(BF16),
        fox_w_in=fox_w_in.astype(BF16),
        fox_w_f=_pad_last(fox_w_in[:, :, 4 * FOX_D:], V7X_LANES).astype(BF16),
        fox_b_f=_pad_last(fox_b_f, V7X_LANES), fox_q_norm=fox_q_norm, fox_k_norm=fox_k_norm,
        fox_w_out=fox_w_out.astype(BF16),
    )
    ffn1 = (ffn1_norm[:, None, :], ffn1_gate, ffn1_up, ffn1_down)
    ffn2 = (ffn2_norm[:, None, :], ffn2_gate, ffn2_up, ffn2_down)
    h = x.reshape(b * t, d)
    for layer in range(DEPTH):
        h = _ffn(h, *ffn1, layer)
        if layer % 2 == 0:
            h = _mixer_gla_conv(h, layer // 2, layer, p)
        else:
            h = _mixer_fox(h, layer // 2, layer, p)
        h = _ffn(h, *ffn2, layer)
    return h.reshape(b, t, d)
```

```python
import functools

import jax
import jax.numpy as jnp
from jax import lax
from jax.experimental import pallas as pl
from jax.experimental.pallas import tpu as pltpu

F32 = jnp.float32
BF16 = jnp.bfloat16

D_MODEL = 2048
SEQ = 8192
DEPTH = 4
D_FF = 5632
GLA_HEADS = 4
GLA_DK = 128
GLA_DV = 256
GLA_QK = GLA_HEADS * GLA_DK
GLA_V = GLA_HEADS * GLA_DV
GLA_RANK = 16
GLA_GATE_TAU = 16.0
GLA_CHUNK = 64
CONV_DIM = D_MODEL // 2
CONV_WIDTH = 31
AB_MAIN = 2 * GLA_QK + 2 * GLA_V
FOX_HEADS = 16
FOX_DH = 128
FOX_D = FOX_HEADS * FOX_DH
EPS = 1e-6

V7X_LANES = 128
V7X_SUBLANES = 8
V7X_VMEM_BYTES = 64 * 1024 * 1024
VMEM_LIMIT = V7X_VMEM_BYTES * 7 // 8

CONV_HALO = 32


def _params(semantics, vmem=VMEM_LIMIT):
    return pltpu.CompilerParams(dimension_semantics=semantics, vmem_limit_bytes=vmem)


def _rms_normed(x, gain):
    ms = jnp.mean(x * x, axis=-1, keepdims=True)
    return x * lax.rsqrt(ms + EPS) * gain


def _log_sigmoid(x):
    return jnp.minimum(x, 0.0) - jnp.log1p(jnp.exp(-jnp.abs(x)))


def _silu(x):
    return x * jax.nn.sigmoid(x)


def _lower_tri(n):
    r = lax.broadcasted_iota(jnp.int32, (n, n), 0)
    c = lax.broadcasted_iota(jnp.int32, (n, n), 1)
    return r >= c


FFN_TM = 1024
FFN_TF = 512
FFN_HEAD_TF = 256


def _ffn_body(h_ref, gain_ref, load_weights, o_ref, xn_ref):
    f = pl.program_id(1)

    @pl.when(f == 0)
    def _():
        xn_ref[...] = _rms_normed(h_ref[...], gain_ref[...]).astype(BF16)
        o_ref[...] = jnp.zeros_like(o_ref)

    wg, wu, wd = load_weights()
    x = xn_ref[...]
    g = jnp.dot(x, wg, preferred_element_type=F32)
    u = jnp.dot(x, wu, preferred_element_type=F32)
    a = (_silu(g) * u).astype(BF16)
    o_ref[...] += jnp.dot(a, wd, preferred_element_type=F32)

    @pl.when(f == pl.num_programs(1) - 1)
    def _():
        o_ref[...] = h_ref[...] + 0.5 * o_ref[...]


FFN_HEAD_SLOTS = 3


def _ffn_head_kernel(layer, h_ref, gain_ref, wg_hbm, wu_hbm, wd_hbm, o_ref, wg_out, wu_out, wd_out,
                     xn_ref, gbuf, ubuf, dbuf, sems):
    f = pl.program_id(1)
    nf = pl.num_programs(1)
    tfh = FFN_HEAD_TF

    def copies(t):
        slot = t % FFN_HEAD_SLOTS
        c0 = pl.multiple_of(t * tfh, tfh)
        return (
            pltpu.make_async_copy(wg_hbm.at[layer, :, pl.ds(c0, tfh)], gbuf.at[slot], sems.at[0, slot]),
            pltpu.make_async_copy(wu_hbm.at[layer, :, pl.ds(c0, tfh)], ubuf.at[slot], sems.at[1, slot]),
            pltpu.make_async_copy(wd_hbm.at[layer, pl.ds(c0, tfh), :], dbuf.at[slot], sems.at[2, slot]),
        )

    @pl.when(f == 0)
    def _():
        for t in range(FFN_HEAD_SLOTS - 1):
            for c in copies(t):
                c.start()

    @pl.when(f + (FFN_HEAD_SLOTS - 1) < nf)
    def _():
        for c in copies(f + (FFN_HEAD_SLOTS - 1)):
            c.start()

    def load_weights():
        for c in copies(f):
            c.wait()
        slot = f % FFN_HEAD_SLOTS
        wg, wu, wd = gbuf[slot].astype(BF16), ubuf[slot].astype(BF16), dbuf[slot].astype(BF16)
        wg_out[...] = wg
        wu_out[...] = wu
        wd_out[...] = wd
        return wg, wu, wd

    _ffn_body(h_ref, gain_ref, load_weights, o_ref, xn_ref)


def _ffn_tail_kernel(h_ref, gain_ref, wg_ref, wu_ref, wd_ref, head_ref, o_ref, xn_ref, sem):
    i = pl.program_id(0)

    @pl.when(jnp.logical_and(i == 0, pl.program_id(1) == 0))
    def _():
        copy = pltpu.make_async_copy(head_ref, o_ref, sem)
        copy.start()
        copy.wait()

    @pl.when(i > 0)
    def _():
        _ffn_body(h_ref, gain_ref, lambda: (wg_ref[...], wu_ref[...], wd_ref[...]), o_ref, xn_ref)


def _ffn(h, gain, wg, wu, wd, layer):
    m, d = h.shape
    dff = wg.shape[2]
    tfh = FFN_HEAD_TF
    once = pl.Buffered(1)
    head, wg16, wu16, wd16 = pl.pallas_call(
        functools.partial(_ffn_head_kernel, layer),
        grid=(1, dff // tfh),
        in_specs=[
            pl.BlockSpec((FFN_TM, d), lambda i, f: (0, 0), pipeline_mode=once),
            pl.BlockSpec((None, 1, d), lambda i, f: (layer, 0, 0)),
            pl.BlockSpec(memory_space=pl.ANY),
            pl.BlockSpec(memory_space=pl.ANY),
            pl.BlockSpec(memory_space=pl.ANY),
        ],
        out_specs=[
            pl.BlockSpec((FFN_TM, d), lambda i, f: (0, 0), pipeline_mode=once),
            pl.BlockSpec((d, tfh), lambda i, f: (0, f)),
            pl.BlockSpec((d, tfh), lambda i, f: (0, f)),
            pl.BlockSpec((tfh, d), lambda i, f: (f, 0)),
        ],
        out_shape=[
            jax.ShapeDtypeStruct((FFN_TM, d), F32),
            jax.ShapeDtypeStruct((d, dff), BF16),
            jax.ShapeDtypeStruct((d, dff), BF16),
            jax.ShapeDtypeStruct((dff, d), BF16),
        ],
        scratch_shapes=[
            pltpu.VMEM((FFN_TM, d), BF16),
            pltpu.VMEM((FFN_HEAD_SLOTS, d, tfh), F32),
            pltpu.VMEM((FFN_HEAD_SLOTS, d, tfh), F32),
            pltpu.VMEM((FFN_HEAD_SLOTS, tfh, d), F32),
            pltpu.SemaphoreType.DMA((3, FFN_HEAD_SLOTS)),
        ],
        compiler_params=_params(("arbitrary", "arbitrary")),
        name="ffn_head",
    )(h, gain, wg, wu, wd)
    wcol = lambda i, f: (0, jnp.where(i == 0, 0, f))
    return pl.pallas_call(
        _ffn_tail_kernel,
        grid=(m // FFN_TM, dff // FFN_TF),
        in_specs=[
            pl.BlockSpec((FFN_TM, d), lambda i, f: (jnp.maximum(i, 1), 0)),
            pl.BlockSpec((None, 1, d), lambda i, f: (layer, 0, 0)),
            pl.BlockSpec((d, FFN_TF), wcol),
            pl.BlockSpec((d, FFN_TF), wcol),
            pl.BlockSpec((FFN_TF, d), lambda i, f: (jnp.where(i == 0, 0, f), 0)),
            pl.BlockSpec(memory_space=pl.ANY),
        ],
        out_specs=pl.BlockSpec((FFN_TM, d), lambda i, f: (i, 0)),
        out_shape=jax.ShapeDtypeStruct((m, d), F32),
        scratch_shapes=[pltpu.VMEM((FFN_TM, d), BF16), pltpu.SemaphoreType.DMA(())],
        compiler_params=_params(("arbitrary", "arbitrary")),
        name="ffn",
    )(h, gain, wg16, wu16, wd16, head)


PROJ_TM = 1024
PROJ_TN = 1024


def _proj_kernel(has_side, h_ref, gain_ref, w_ref, *rest):
    if has_side:
        ws_ref, o_ref, os_ref, xn_ref = rest
    else:
        o_ref, xn_ref = rest

    @pl.when(pl.program_id(1) == 0)
    def _():
        xn_ref[...] = _rms_normed(h_ref[...], gain_ref[...]).astype(BF16)
        if has_side:
            os_ref[...] = jnp.dot(xn_ref[...], ws_ref[...], preferred_element_type=F32)

    o_ref[...] = jnp.dot(xn_ref[...], w_ref[...], preferred_element_type=F32)


def _proj(h, gain, layer, w, idx, n, w_side=None):
    m, d = h.shape
    in_specs = [
        pl.BlockSpec((PROJ_TM, d), lambda i, j: (i, 0)),
        pl.BlockSpec((None, 1, d), lambda i, j: (layer, 0, 0)),
        pl.BlockSpec((None, d, PROJ_TN), lambda i, j: (idx, 0, j)),
    ]
    out_specs = [pl.BlockSpec((PROJ_TM, PROJ_TN), lambda i, j: (i, j))]
    out_shape = [jax.ShapeDtypeStruct((m, n), F32)]
    args = [h, gain, w]
    if w_side is not None:
        ns = w_side.shape[2]
        in_specs.append(pl.BlockSpec((None, d, ns), lambda i, j: (idx, 0, 0)))
        out_specs.append(pl.BlockSpec((PROJ_TM, ns), lambda i, j: (i, 0)))
        out_shape.append(jax.ShapeDtypeStruct((m, ns), F32))
        args.append(w_side)
    return pl.pallas_call(
        functools.partial(_proj_kernel, w_side is not None),
        grid=(m // PROJ_TM, n // PROJ_TN),
        in_specs=in_specs,
        out_specs=out_specs,
        out_shape=out_shape,
        scratch_shapes=[pltpu.VMEM((PROJ_TM, d), BF16)],
        compiler_params=_params(("parallel", "arbitrary")),
        name="proj",
    )(*args)


OUT_TM = 512
OUT_TN = 2048


def _out_proj_kernel(a_ref, w_ref, h_ref, o_ref):
    o_ref[...] = h_ref[...] + jnp.dot(a_ref[...], w_ref[...], preferred_element_type=F32)


def _out_proj(a, w, h, idx):
    m, k = a.shape
    n = w.shape[2]
    return pl.pallas_call(
        _out_proj_kernel,
        grid=(m // OUT_TM, n // OUT_TN),
        in_specs=[
            pl.BlockSpec((OUT_TM, k), lambda i, j: (i, 0)),
            pl.BlockSpec((None, k, OUT_TN), lambda i, j: (idx, 0, j)),
            pl.BlockSpec((OUT_TM, OUT_TN), lambda i, j: (i, j)),
        ],
        out_specs=pl.BlockSpec((OUT_TM, OUT_TN), lambda i, j: (i, j)),
        out_shape=jax.ShapeDtypeStruct((m, n), F32),
        compiler_params=_params(("parallel", "parallel")),
        name="out_proj",
    )(a, w, h)


MIX_TT = 512
CONV_RB = 128
CONV_LW = 128


def _gla_conv_kernel(q_ref, k_ref, v_ref, g_ref, cv_ref, cg_ref, lr_ref,
                     wgk_ref, bgk_ref, gnorm_ref, cw_ref, cb_ref, lng_ref, lnb_ref,
                     mix_ref, st_ref, cbuf_ref, la_ref):
    @pl.when(pl.program_id(0) == 0)
    def _():
        st_ref[...] = jnp.zeros_like(st_ref)
        cbuf_ref[0:CONV_HALO, :] = jnp.zeros((CONV_HALO, CONV_DIM), F32)

    gk = jnp.dot(lr_ref[...].astype(BF16), wgk_ref[...], preferred_element_type=F32) + bgk_ref[...]
    la_ref[...] = _log_sigmoid(gk) * (1.0 / GLA_GATE_TAU)

    tri = _lower_tri(GLA_CHUNK)
    tri_f = tri.astype(F32)
    gnorm = gnorm_ref[...]
    mid = GLA_CHUNK // 2

    def chunk(c, carry):
        r0 = pl.multiple_of(c * GLA_CHUNK, GLA_CHUNK)
        rows = pl.ds(r0, GLA_CHUNK)
        cum = jnp.dot(tri_f, la_ref[rows, :], preferred_element_type=F32,
                      precision=lax.Precision.HIGHEST)
        for h in range(GLA_HEADS):
            ks = slice(h * GLA_DK, (h + 1) * GLA_DK)
            vs = slice(h * GLA_DV, (h + 1) * GLA_DV)
            cum_h = cum[:, ks]
            ref = cum_h[mid:mid + 1, :]
            last = cum_h[GLA_CHUNK - 1:GLA_CHUNK, :]
            q = q_ref[rows, ks] * (GLA_DK ** -0.5)
            k = k_ref[rows, ks]
            v = v_ref[rows, vs].astype(BF16)
            qt = (q * jnp.exp(cum_h - ref)).astype(BF16)
            kt = (k * jnp.exp(ref - cum_h)).astype(BF16)
            a = lax.dot_general(qt, kt, (((1,), (1,)), ((), ())), preferred_element_type=F32)
            a = jnp.where(tri, a, 0.0).astype(BF16)
            o = jnp.dot(a, v, preferred_element_type=F32)
            st = st_ref[h]
            q_in = (q * jnp.exp(cum_h)).astype(BF16)
            o = o + lax.dot_general(q_in, st.astype(BF16), (((1,), (1,)), ((), ())),
                                    preferred_element_type=F32)
            k_out = (k * jnp.exp(last - cum_h)).astype(BF16)
            st_ref[h] = st * jnp.exp(last) + lax.dot_general(
                v, k_out, (((0,), (0,)), ((), ())), preferred_element_type=F32)
            on = _rms_normed(o, gnorm)
            mix_ref[rows, vs] = (on * _silu(g_ref[rows, vs])).astype(BF16)
        return carry

    lax.fori_loop(0, MIX_TT // GLA_CHUNK, chunk, 0, unroll=True)

    cbuf_ref[CONV_HALO:CONV_HALO + MIX_TT, :] = cv_ref[...] * jax.nn.sigmoid(cg_ref[...])
    cb = cb_ref[...]
    lng = lng_ref[...]
    lnb = lnb_ref[...]
    tap0 = CONV_HALO - (CONV_WIDTH - 1)

    def conv_rows(rb, carry):
        base = pl.multiple_of(rb * CONV_RB, CONV_RB)
        halves = []
        for lane0 in range(0, CONV_DIM, CONV_LW):
            ls = slice(lane0, lane0 + CONV_LW)
            acc = None
            for s in range(V7X_SUBLANES):
                part = None
                rows = CONV_RB + (V7X_SUBLANES if s else 0)
                for d in range(s, tap0 + CONV_WIDTH, V7X_SUBLANES):
                    if d < tap0:
                        continue
                    win = cbuf_ref[pl.ds(base + (d - s), rows), ls]
                    term = cw_ref[d - tap0:d - tap0 + 1, ls] * win
                    part = term if part is None else part + term
                part = part[s:s + CONV_RB, :]
                acc = part if acc is None else acc + part
            halves.append(acc)
        y = jnp.concatenate(halves, axis=1) + cb
        mu = jnp.mean(y, axis=-1, keepdims=True)
        yc = y - mu
        var = jnp.mean(yc * yc, axis=-1, keepdims=True)
        yn = yc * lax.rsqrt(var + EPS) * lng + lnb
        mix_ref[pl.ds(base, CONV_RB), GLA_V:GLA_V + CONV_DIM] = _silu(yn).astype(BF16)
        return carry

    lax.fori_loop(0, MIX_TT // CONV_RB, conv_rows, 0)
    cbuf_ref[0:CONV_HALO, :] = cbuf_ref[MIX_TT:MIX_TT + CONV_HALO, :]


def _gla_conv(z, zc, zlr, wgk, bgk, gnorm, cw, cb, lng, lnb):
    m = z.shape[0]
    row = lambda shape: pl.BlockSpec(shape, lambda t: (0, 0))
    return pl.pallas_call(
        _gla_conv_kernel,
        grid=(m // MIX_TT,),
        in_specs=[
            pl.BlockSpec((MIX_TT, GLA_QK), lambda t: (t, 0)),
            pl.BlockSpec((MIX_TT, GLA_QK), lambda t: (t, 1)),
            pl.BlockSpec((MIX_TT, GLA_V), lambda t: (t, 1)),
            pl.BlockSpec((MIX_TT, GLA_V), lambda t: (t, 2)),
            pl.BlockSpec((MIX_TT, CONV_DIM), lambda t: (t, 0)),
            pl.BlockSpec((MIX_TT, CONV_DIM), lambda t: (t, 1)),
            pl.BlockSpec((MIX_TT, V7X_LANES), lambda t: (t, 0)),
            row((V7X_LANES, GLA_QK)),
            row((1, GLA_QK)),
            row((1, GLA_DV)),
            row((CONV_WIDTH, CONV_DIM)),
            row((1, CONV_DIM)),
            row((1, CONV_DIM)),
            row((1, CONV_DIM)),
        ],
        out_specs=pl.BlockSpec((MIX_TT, GLA_V + CONV_DIM), lambda t: (t, 0)),
        out_shape=jax.ShapeDtypeStruct((m, GLA_V + CONV_DIM), BF16),
        scratch_shapes=[
            pltpu.VMEM((GLA_HEADS, GLA_DV, GLA_DK), F32),
            pltpu.VMEM((CONV_HALO + MIX_TT, CONV_DIM), F32),
            pltpu.VMEM((MIX_TT, GLA_QK), F32),
        ],
        compiler_params=_params(("arbitrary",)),
        name="gla_conv",
    )(z, z, z, z, zc, zc, zlr, wgk, bgk, gnorm, cw, cb, lng, lnb)


PREP_TM = 256
CUM_TB = 256
ATT_TQ = 1024
ATT_TK = 512
FOX_DP = 2 * FOX_DH
LOG2E = 1.4426950408889634


def _split3(c):
    hi = c.astype(BF16).astype(F32)
    r = c - hi
    mid = r.astype(BF16).astype(F32)
    lo = (r - mid).astype(BF16).astype(F32)
    return hi, mid, lo


def _fox_prep_kernel(zq_ref, zk_ref, zv_ref, cum_ref, qg_ref, kg_ref, qp_ref, kp_ref, vp_ref):
    qg = qg_ref[...] * (FOX_DH ** -0.5 * LOG2E)
    kg = kg_ref[...]
    lane = lax.broadcasted_iota(jnp.int32, (PREP_TM, V7X_LANES), 1)
    ones = jnp.ones((PREP_TM, V7X_LANES), BF16)
    cum = cum_ref[...] * LOG2E
    for h in range(FOX_HEADS):
        hs = slice(h * FOX_DH, (h + 1) * FOX_DH)
        f0 = h * FOX_DP
        hi, mid, lo = _split3(jnp.broadcast_to(cum[:, h:h + 1], (PREP_TM, V7X_LANES)))
        qb = jnp.where(lane == 0, hi, jnp.where(lane == 1, mid, jnp.where(lane == 2, lo,
             jnp.where(lane < 6, 1.0, 0.0))))
        kb = jnp.where(lane < 3, 1.0, jnp.where(lane == 3, -hi, jnp.where(lane == 4, -mid,
             jnp.where(lane == 5, -lo, 0.0))))
        qp_ref[:, f0:f0 + FOX_DH] = _rms_normed(zq_ref[:, hs], qg).astype(BF16)
        qp_ref[:, f0 + FOX_DH:f0 + FOX_DP] = qb.astype(BF16)
        kp_ref[:, f0:f0 + FOX_DH] = _rms_normed(zk_ref[:, hs], kg).astype(BF16)
        kp_ref[:, f0 + FOX_DH:f0 + FOX_DP] = kb.astype(BF16)
        vp_ref[:, f0:f0 + FOX_DH] = zv_ref[:, hs].astype(BF16)
        vp_ref[:, f0 + FOX_DH:f0 + FOX_DP] = ones


def _fox_prep(z, cum, qg, kg):
    m = z.shape[0]
    blk = lambda c: pl.BlockSpec((PREP_TM, FOX_D), lambda i, c=c: (i, c))
    vec = pl.BlockSpec((1, FOX_DH), lambda i: (0, 0))
    out_spec = pl.BlockSpec((PREP_TM, FOX_HEADS * FOX_DP), lambda i: (i, 0))
    out = jax.ShapeDtypeStruct((m, FOX_HEADS * FOX_DP), BF16)
    return pl.pallas_call(
        _fox_prep_kernel,
        grid=(m // PREP_TM,),
        in_specs=[blk(0), blk(1), blk(2), pl.BlockSpec((PREP_TM, V7X_LANES), lambda i: (i, 0)), vec, vec],
        out_specs=[out_spec, out_spec, out_spec],
        out_shape=[out, out, out],
        compiler_params=_params(("parallel",)),
        name="fox_prep",
    )(z, z, z, cum, qg, kg)


def _fox_cum_kernel(f_ref, b_ref, o_ref, carry_ref):
    @pl.when(pl.program_id(0) == 0)
    def _():
        carry_ref[...] = jnp.zeros_like(carry_ref)

    lf = _log_sigmoid(f_ref[...] + b_ref[...])
    cum = jnp.dot(_lower_tri(CUM_TB).astype(F32), lf, preferred_element_type=F32,
                  precision=lax.Precision.HIGHEST) + carry_ref[0:1, :]
    o_ref[...] = cum
    carry_ref[...] = jnp.broadcast_to(cum[CUM_TB - 1:CUM_TB, :], carry_ref.shape)


def _fox_cum(zf, bf):
    m, n = zf.shape
    return pl.pallas_call(
        _fox_cum_kernel,
        grid=(m // CUM_TB,),
        in_specs=[pl.BlockSpec((CUM_TB, n), lambda i: (i, 0)), pl.BlockSpec((1, n), lambda i: (0, 0))],
        out_specs=pl.BlockSpec((CUM_TB, n), lambda i: (i, 0)),
        out_shape=jax.ShapeDtypeStruct((m, n), F32),
        scratch_shapes=[pltpu.VMEM((V7X_SUBLANES, n), F32)],
        compiler_params=_params(("arbitrary",)),
        name="fox_cum",
    )(zf, bf)


def _fox_attn_kernel(q_ref, k_ref, v_ref, og_ref, o_ref, m_ref, acc_ref):
    i = pl.program_id(1)
    q = q_ref[...]
    m_ref[...] = jnp.full_like(m_ref, -jnp.inf)
    acc_ref[...] = jnp.zeros_like(acc_ref)

    def logits(j, rows=slice(None)):
        k0 = pl.multiple_of(j * ATT_TK, ATT_TK)
        return lax.dot_general(q[rows], k_ref[pl.ds(k0, ATT_TK), :], (((1,), (1,)), ((), ())),
                               preferred_element_type=F32)

    def update(s, j, rows=slice(None)):
        k0 = pl.multiple_of(j * ATT_TK, ATT_TK)
        m_prev = m_ref[rows, :]
        m_new = jnp.maximum(m_prev, jnp.max(s, axis=-1, keepdims=True))
        alpha = jnp.exp2(m_prev - m_new)
        p = jnp.exp2(s - jnp.concatenate([m_new] * (ATT_TK // V7X_LANES), axis=1))
        pv = jnp.dot(p.astype(BF16), v_ref[pl.ds(k0, ATT_TK), :], preferred_element_type=F32)
        acc_ref[rows, :] = jnp.concatenate([alpha] * (FOX_DP // V7X_LANES), axis=1) * acc_ref[rows, :] + pv
        m_ref[rows, :] = m_new

    def pair(jj, carry):
        s_a = logits(2 * jj)
        s_b = logits(2 * jj + 1)
        update(s_a, 2 * jj)
        update(s_b, 2 * jj + 1)
        return carry

    lax.fori_loop(0, i, pair, 0)

    row = lax.broadcasted_iota(jnp.int32, (ATT_TQ, ATT_TK), 0)
    col = lax.broadcasted_iota(jnp.int32, (ATT_TQ, ATT_TK), 1)
    update(jnp.where(row >= col, logits(2 * i), -jnp.inf), 2 * i)
    late = slice(ATT_TK, ATT_TQ)
    update(jnp.where(_lower_tri(ATT_TK), logits(2 * i + 1, late), -jnp.inf), 2 * i + 1, late)
    acc = acc_ref[...]
    o = acc[:, :FOX_DH] / acc[:, FOX_DH:]
    o_ref[...] = (o * jax.nn.sigmoid(og_ref[...])).astype(BF16)


def _fox_attn(qp, kp, vp, z):
    m = qp.shape[0]
    og_col0 = 3 * FOX_HEADS
    return pl.pallas_call(
        _fox_attn_kernel,
        grid=(FOX_HEADS, m // ATT_TQ),
        in_specs=[
            pl.BlockSpec((ATT_TQ, FOX_DP), lambda h, i: (i, h)),
            pl.BlockSpec((m, FOX_DP), lambda h, i: (0, h)),
            pl.BlockSpec((m, FOX_DP), lambda h, i: (0, h)),
            pl.BlockSpec((ATT_TQ, FOX_DH), lambda h, i: (i, og_col0 + h)),
        ],
        out_specs=pl.BlockSpec((ATT_TQ, FOX_DH), lambda h, i: (i, h)),
        out_shape=jax.ShapeDtypeStruct((m, FOX_D), BF16),
        scratch_shapes=[
            pltpu.VMEM((ATT_TQ, V7X_LANES), F32),
            pltpu.VMEM((ATT_TQ, FOX_DP), F32),
        ],
        compiler_params=_params(("parallel", "arbitrary")),
        name="fox_attn",
    )(qp, kp, vp, z)


def _pad_last(w, n):
    return jnp.pad(w, [(0, 0)] * (w.ndim - 1) + [(0, n - w.shape[-1])])


def _mixer_gla_conv(h, e, layer, p):
    z, zlr = _proj(h, p["mix_norm"], layer, p["ab_w_in"], e, AB_MAIN, p["ab_w_lr"])
    zc, = _proj(h, p["mix_norm"], layer, p["ab_w_conv"], e, 2 * CONV_DIM)
    mix = _gla_conv(z, zc, zlr, p["gla_w_gk2"][e], p["gla_b_gk"][e][None, :], p["gla_out_norm"][e][None, :],
                    p["conv_w"][e], p["conv_b"][e][None, :], p["conv_ln_g"][e][None, :],
                    p["conv_ln_b"][e][None, :])
    return _out_proj(mix, p["ab_w_out"], h, e)


def _mixer_fox(h, o, layer, p):
    z, zf = _proj(h, p["mix_norm"], layer, p["fox_w_in"], o, 4 * FOX_D, p["fox_w_f"])
    cum = _fox_cum(zf, p["fox_b_f"][o][None, :])
    qp, kp, vp = _fox_prep(z, cum, p["fox_q_norm"][o][None, :], p["fox_k_norm"][o][None, :])
    att = _fox_attn(qp, kp, vp, z)
    return _out_proj(att, p["fox_w_out"], h, o)


def kernel(x, ffn1_norm, ffn1_gate, ffn1_up, ffn1_down, mix_norm, ffn2_norm, ffn2_gate, ffn2_up,
           ffn2_down, ab_w_in, gla_w_gk2, gla_b_gk, gla_out_norm, conv_w, conv_b, conv_ln_g,
           conv_ln_b, ab_w_out, fox_w_in, fox_b_f, fox_q_norm, fox_k_norm, fox_w_out):
    b, t, d = x.shape
    assert b == 1, "the time recurrences assume one sequence"
    lr0 = AB_MAIN
    p = dict(
        mix_norm=mix_norm[:, None, :],
        ab_w_in=ab_w_in.astype(BF16),
        ab_w_conv=ab_w_in[:, :, lr0 + GLA_RANK:].astype(BF16),
        ab_w_lr=_pad_last(ab_w_in[:, :, lr0:lr0 + GLA_RANK], V7X_LANES).astype(BF16),
        gla_w_gk2=jnp.pad(gla_w_gk2, ((0, 0), (0, V7X_LANES - GLA_RANK), (0, 0))).astype(BF16),
        gla_b_gk=gla_b_gk, gla_out_norm=gla_out_norm, conv_w=conv_w, conv_b=conv_b,
        conv_ln_g=conv_ln_g, conv_ln_b=conv_ln_b, ab_w_out=ab_w_out.astype(BF16),
        fox_w_in=fox_w_in.astype(BF16),
        fox_w_f=_pad_last(fox_w_in[:, :, 4 * FOX_D:], V7X_LANES).astype(BF16),
        fox_b_f=_pad_last(fox_b_f, V7X_LANES), fox_q_norm=fox_q_norm, fox_k_norm=fox_k_norm,
        fox_w_out=fox_w_out.astype(BF16),
    )
    ffn1 = (ffn1_norm[:, None, :], ffn1_gate, ffn1_up, ffn1_down)
    ffn2 = (ffn2_norm[:, None, :], ffn2_gate, ffn2_up, ffn2_down)
    h = x.reshape(b * t, d)
    for layer in range(DEPTH):
        h = _ffn(h, *ffn1, layer)
        if layer % 2 == 0:
            h = _mixer_gla_conv(h, layer // 2, layer, p)
        else:
            h = _mixer_fox(h, layer // 2, layer, p)
        h = _ffn(h, *ffn2, layer)
    return h.reshape(b, t, d)
```
